```python
import math
import jax, jax.numpy as jnp
from jax import lax
import numpy as np

D_MODEL = 1024
BATCH = 8
SEQ = 4096
DEPTH = 2
DEC_BATCH = 4
DEC_SEQ = 4096
PAST_LEN = 128

D_HY = 512
HY_SHORT = 3
HY_EMB_BANDS = 8
HY_EMB = 1 + 2 * HY_EMB_BANDS
HY_FILT_HID = 64
HY_FAST_DECAY = 0.3
HY_SLOW_DECAY = 1.5
HY_TARGET = 1e-2
HEAD_DIM = 64
HEADS_PER_GROUP = 4
DIL_PATTERNS = ((128, 1), (512, 4), (2048, 16))
N_GROUPS = len(DIL_PATTERNS)
N_ATT_HEADS = N_GROUPS * HEADS_PER_GROUP
D_ATT = N_ATT_HEADS * HEAD_DIM
D_ATT_OUT = HEADS_PER_GROUP * HEAD_DIM
ATT_BLOCK = 64
ROPE_THETA = 10000.0
NEG_INF = -1e30
DEN_FLOOR = 1e-30
D_RG = 512
RG_BLOCKS = 8
RG_BLOCK_DIM = D_RG // RG_BLOCKS
RG_CONV = 4
RG_C = 8.0
D_FF = 3 * D_MODEL
FFN_CONV = 3
DN_ALPHA = (2 * DEPTH) ** 0.25
DN_BETA = (8 * DEPTH) ** -0.25
LN_EPS = 1e-5
D_IN = 3 * D_HY + 3 * D_ATT + 2 * D_RG

kernel_name = 'hybrid_hyena_dilattn_rglru_encoder'


def layer_norm(x, g, b):
    xf = x.astype(jnp.float32)
    mu = jnp.mean(xf, axis=-1, keepdims=True)
    var = jnp.mean(jnp.square(xf - mu), axis=-1, keepdims=True)
    y = (xf - mu) * lax.rsqrt(var + LN_EPS)
    return (y * g.astype(jnp.float32) + b.astype(jnp.float32)).astype(x.dtype)


def depthwise_conv(x, w, b, pad_left, pad_right):
    c = x.shape[-1]
    y = lax.conv_general_dilated(x, w.astype(x.dtype)[:, None, :], window_strides=(1,),
                                 padding=[(pad_left, pad_right)],
                                 dimension_numbers=('NWC', 'WIO', 'NWC'), feature_group_count=c)
    return y + b.astype(x.dtype)


def hyena_filters(L, w1, b1, w2, b2, w3, b3, freq):
    f32 = jnp.float32
    t = jnp.linspace(0.0, 1.0, L, dtype=f32)[:, None]
    w = 2.0 * math.pi * jnp.arange(L, dtype=f32)[:, None] / L
    bands = jnp.linspace(1e-4, HY_EMB_BANDS - 1, HY_EMB_BANDS, dtype=f32)[None, :]
    z = jnp.concatenate([t, jnp.cos(bands * w), -jnp.sin(bands * w)], axis=-1)
    fr = freq.astype(f32)
    h = jnp.sin(fr * (z @ w1.astype(f32) + b1.astype(f32)))
    h = jnp.sin(fr * (h @ w2.astype(f32) + b2.astype(f32)))
    h = h @ w3.astype(f32) + b3.astype(f32)
    deltas = jnp.abs(jnp.linspace(math.log(HY_TARGET) / HY_SLOW_DECAY,
                                  math.log(HY_TARGET) / HY_FAST_DECAY, D_HY, dtype=f32))
    decay = jnp.exp(-t * deltas[None, :])
    h_fwd = h[:, :D_HY] * decay
    h_bwd = h[:, D_HY:] * decay
    k = jnp.concatenate([h_fwd, jnp.zeros((1, D_HY), f32), h_bwd[:0:-1]], axis=0)
    return k / jnp.sum(jnp.abs(k), axis=0, keepdims=True)


def hyena_long_conv(u, k, bias):
    L = u.shape[1]
    U = jnp.fft.rfft(u, n=2 * L, axis=1)
    K = jnp.fft.rfft(k, n=2 * L, axis=0)
    y = jnp.fft.irfft(U * K[None], n=2 * L, axis=1)[:, :L]
    return y + u * bias


def hyena_mixer(xa, conv_w, conv_b, w1, b1, w2, b2, w3, b3, freq, bias):
    L = xa.shape[1]
    u = depthwise_conv(xa, conv_w, conv_b, 1, 1).astype(jnp.float32)
    x0, x1, v = jnp.split(u, 3, axis=-1)
    k = hyena_filters(L, w1, b1, w2, b2, w3, b3, freq)
    z = hyena_long_conv(v * x1, k, bias.astype(jnp.float32))
    return (x0 * z).astype(xa.dtype)


def rope(t):
    L, dh = t.shape[1], t.shape[-1]
    inv = ROPE_THETA ** (-jnp.arange(0, dh, 2, dtype=jnp.float32) / dh)
    ang = jnp.arange(L, dtype=jnp.float32)[:, None] * inv[None, :]
    cos = jnp.cos(ang)[None, :, None, :]
    sin = jnp.sin(ang)[None, :, None, :]
    t1, t2 = t[..., :dh // 2], t[..., dh // 2:]
    return jnp.concatenate([t1 * cos - t2 * sin, t2 * cos + t1 * sin], axis=-1)


def strided_window_attention(q, k, v, dil, radius):
    B, S, H, dh = q.shape
    W = ATT_BLOCK
    n = S // dil
    nb = -(-n // W)
    n_pad = nb * W

    def to_sub(t):
        t = t.reshape(B, n, dil, H, dh).transpose(0, 2, 1, 3, 4)
        return jnp.pad(t, ((0, 0), (0, 0), (0, n_pad - n), (0, 0), (0, 0)))

    def windows(t):
        t = jnp.pad(t, ((0, 0), (0, 0), (W, W), (0, 0), (0, 0))).reshape(B, dil, nb + 2, W, H, dh)
        return jnp.concatenate([t[:, :, :-2], t[:, :, 1:-1], t[:, :, 2:]], axis=3)

    qb = to_sub(q).reshape(B, dil, nb, W, H, dh)
    kw = windows(to_sub(k))
    vw = windows(to_sub(v))
    s = jnp.einsum('bdnqhe,bdnkhe->bdnhqk', qb, kw) * (dh ** -0.5)
    qi = jnp.arange(nb)[:, None] * W + jnp.arange(W)[None, :]
    kj = (jnp.arange(nb)[:, None] - 1) * W + jnp.arange(3 * W)[None, :]
    rel = kj[:, None, :] - qi[:, :, None]
    mask = (jnp.abs(rel) <= radius) & (kj[:, None, :] >= 0) & (kj[:, None, :] < n)
    mask = mask[:, None]
    s = jnp.where(mask, s, NEG_INF)
    m = jnp.max(s, axis=-1, keepdims=True)
    e = jnp.where(mask, jnp.exp(s - m), 0.0)
    den = jnp.maximum(jnp.sum(e, axis=-1, keepdims=True), DEN_FLOOR)
    o = jnp.einsum('bdnhqk,bdnkhe->bdnqhe', e / den, vw)
    lse = (m + jnp.log(den))[..., 0]
    o = o.reshape(B, dil, n_pad, H, dh)[:, :, :n].transpose(0, 2, 1, 3, 4).reshape(B, S, H, dh)
    lse = lse.transpose(0, 1, 2, 4, 3).reshape(B, dil, n_pad, H)[:, :, :n]
    lse = lse.transpose(0, 2, 1, 3).reshape(B, S, H)
    return o, lse


def dilated_attention(q, k, v):
    outs, lses = [], []
    for g, (window, dil) in enumerate(DIL_PATTERNS):
        o, lse = strided_window_attention(q[:, :, g], k[:, :, g], v[:, :, g], dil, window // (2 * dil))
        outs.append(o)
        lses.append(lse)
    o = jnp.stack(outs, axis=2)
    wgt = jax.nn.softmax(jnp.stack(lses, axis=2), axis=2)
    return jnp.einsum('blgh,blghe->blhe', wgt, o)


def _lru_combine(c1, c2):
    a1, b1 = c1
    a2, b2 = c2
    return a1 * a2, a2 * b1 + b2


def rglru_scan(x, gate_w, gate_b, lam):
    B, L, D = x.shape
    xb = x.reshape(B, L, RG_BLOCKS, RG_BLOCK_DIM)
    g = jnp.einsum('blnd,gnde->gblne', xb, gate_w.astype(jnp.float32)).reshape(2, B, L, D)
    g = g + gate_b.astype(jnp.float32)[:, None, None, :]
    r = jax.nn.sigmoid(g[0])
    i = jax.nn.sigmoid(g[1])
    log_a = -RG_C * r * jax.nn.softplus(-lam.astype(jnp.float32))
    a = jnp.exp(log_a)
    mult = jnp.sqrt(-jnp.expm1(2.0 * log_a))
    mult = mult.at[:, 0].set(1.0)
    xn = x * i * mult
    _, h = lax.associative_scan(_lru_combine, (a, xn), axis=1)
    return h


def rglru_mixer(xc, conv_w, conv_b, gate_w, gate_b, lam):
    xr, gate = jnp.split(xc, 2, axis=-1)
    xr = depthwise_conv(xr, conv_w, conv_b, 2, 1).astype(jnp.float32)
    h_f = rglru_scan(xr, gate_w[0], gate_b[0], lam[0])
    h_b = jnp.flip(rglru_scan(jnp.flip(xr, axis=1), gate_w[1], gate_b[1], lam[1]), axis=1)
    return ((h_f + h_b) * jax.nn.gelu(gate.astype(jnp.float32))).astype(xc.dtype)


def encoder_layer(x, w_in, hy_conv_w, hy_conv_b, hy_filt_w1, hy_filt_b1, hy_filt_w2, hy_filt_b2,
                  hy_filt_w3, hy_filt_b3, hy_filt_freq, hy_bias, rg_conv_w, rg_conv_b, rg_gate_w,
                  rg_gate_b, rg_lam, w_gate, b_gate, w_br_a, w_br_b, w_br_c, w_o, ln1_g, ln1_b,
                  w_up, ffn_conv_w, ffn_conv_b, w_down, ln2_g, ln2_b):
    B, L, _ = x.shape
    dt = x.dtype
    proj = x @ w_in
    xa, xq, xk, xv, xc = jnp.split(
        proj, [3 * D_HY, 3 * D_HY + D_ATT, 3 * D_HY + 2 * D_ATT, 3 * D_HY + 3 * D_ATT], axis=-1)
    ya = hyena_mixer(xa, hy_conv_w, hy_conv_b, hy_filt_w1, hy_filt_b1, hy_filt_w2, hy_filt_b2,
                     hy_filt_w3, hy_filt_b3, hy_filt_freq, hy_bias)
    q = rope(xq.astype(jnp.float32).reshape(B, L, N_ATT_HEADS, HEAD_DIM))
    k = rope(xk.astype(jnp.float32).reshape(B, L, N_ATT_HEADS, HEAD_DIM))
    v = xv.astype(jnp.float32).reshape(B, L, N_ATT_HEADS, HEAD_DIM)
    gshape = (B, L, N_GROUPS, HEADS_PER_GROUP, HEAD_DIM)
    yb = dilated_attention(q.reshape(gshape), k.reshape(gshape), v.reshape(gshape))
    yb = yb.reshape(B, L, D_ATT_OUT).astype(dt)
    yc = rglru_mixer(xc, rg_conv_w, rg_conv_b, rg_gate_w, rg_gate_b, rg_lam)
    gates = jax.nn.sigmoid((x @ w_gate + b_gate).astype(jnp.float32)).reshape(B, L, 3, D_MODEL)
    mixed = (gates[:, :, 0] * (ya @ w_br_a).astype(jnp.float32)
             + gates[:, :, 1] * (yb @ w_br_b).astype(jnp.float32)
             + gates[:, :, 2] * (yc @ w_br_c).astype(jnp.float32)).astype(dt)
    x = layer_norm(DN_ALPHA * x + mixed @ w_o, ln1_g, ln1_b)
    hg, hu = jnp.split(x @ w_up, 2, axis=-1)
    hg = depthwise_conv(hg, ffn_conv_w, ffn_conv_b, 1, 1)
    f = (jax.nn.gelu(hg) * hu) @ w_down
    return layer_norm(DN_ALPHA * x + f, ln2_g, ln2_b)


def setup_inputs(seed: int = 0) -> dict:
    key = jax.random.key(seed)
    ks = iter(jax.random.split(key, 40))
    f32 = jnp.float32

    def nrm(shape, scale):
        return jax.random.normal(next(ks), shape, f32) * scale

    x_prompt = nrm((BATCH, SEQ, D_MODEL), 1.0)
    x_sample = nrm((DEC_BATCH, DEC_SEQ, D_MODEL), 1.0)
    w_in = nrm((DEPTH, D_MODEL, D_IN), D_MODEL ** -0.5)
    hy_conv_w = nrm((DEPTH, HY_SHORT, 3 * D_HY), HY_SHORT ** -0.5)
    hy_conv_b = nrm((DEPTH, 3 * D_HY), 0.02)
    hy_filt_w1 = nrm((DEPTH, HY_EMB, HY_FILT_HID), HY_EMB ** -0.5)
    hy_filt_b1 = nrm((DEPTH, HY_FILT_HID), 0.02)
    hy_filt_w2 = nrm((DEPTH, HY_FILT_HID, HY_FILT_HID), HY_FILT_HID ** -0.5)
    hy_filt_b2 = nrm((DEPTH, HY_FILT_HID), 0.02)
    hy_filt_w3 = nrm((DEPTH, HY_FILT_HID, 2 * D_HY), HY_FILT_HID ** -0.5)
    hy_filt_b3 = nrm((DEPTH, 2 * D_HY), 0.02)
    hy_filt_freq = 1.0 + nrm((DEPTH, HY_FILT_HID), 0.01)
    hy_bias = nrm((DEPTH, D_HY), 1.0)
    rg_conv_w = nrm((DEPTH, RG_CONV, D_RG), RG_CONV ** -0.5)
    rg_conv_b = nrm((DEPTH, D_RG), 0.02)
    rg_gate_w = nrm((DEPTH, 2, 2, RG_BLOCKS, RG_BLOCK_DIM, RG_BLOCK_DIM), RG_BLOCK_DIM ** -0.5)
    rg_gate_b = nrm((DEPTH, 2, 2, D_RG), 0.01)
    a_c = jax.random.uniform(next(ks), (DEPTH, 2, D_RG), f32, 0.9, 0.999)
    s = a_c ** (1.0 / RG_C)
    rg_lam = jnp.log(s) - jnp.log1p(-s)
    w_gate = nrm((DEPTH, D_MODEL, 3 * D_MODEL), D_MODEL ** -0.5)
    b_gate = nrm((DEPTH, 3 * D_MODEL), 0.02)
    w_br_a = nrm((DEPTH, D_HY, D_MODEL), D_HY ** -0.5)
    w_br_b = nrm((DEPTH, D_ATT_OUT, D_MODEL), D_ATT_OUT ** -0.5)
    w_br_c = nrm((DEPTH, D_RG, D_MODEL), D_RG ** -0.5)
    w_o = nrm((DEPTH, D_MODEL, D_MODEL), D_MODEL ** -0.5 * DN_BETA)
    ln1_g = 1.0 + nrm((DEPTH, D_MODEL), 0.02)
    ln1_b = nrm((DEPTH, D_MODEL), 0.02)
    w_up = nrm((DEPTH, D_MODEL, 2 * D_FF), D_MODEL ** -0.5)
    ffn_conv_w = nrm((DEPTH, FFN_CONV, D_FF), FFN_CONV ** -0.5)
    ffn_conv_b = nrm((DEPTH, D_FF), 0.02)
    w_down = nrm((DEPTH, D_FF, D_MODEL), D_FF ** -0.5 * DN_BETA)
    ln2_g = 1.0 + nrm((DEPTH, D_MODEL), 0.02)
    ln2_b = nrm((DEPTH, D_MODEL), 0.02)
    return {'x_prompt': x_prompt, 'x_sample': x_sample, 'w_in': w_in,
            'hy_conv_w': hy_conv_w, 'hy_conv_b': hy_conv_b,
            'hy_filt_w1': hy_filt_w1, 'hy_filt_b1': hy_filt_b1, 'hy_filt_w2': hy_filt_w2,
            'hy_filt_b2': hy_filt_b2, 'hy_filt_w3': hy_filt_w3, 'hy_filt_b3': hy_filt_b3,
            'hy_filt_freq': hy_filt_freq, 'hy_bias': hy_bias,
            'rg_conv_w': rg_conv_w, 'rg_conv_b': rg_conv_b, 'rg_gate_w': rg_gate_w,
            'rg_gate_b': rg_gate_b, 'rg_lam': rg_lam,
            'w_gate': w_gate, 'b_gate': b_gate, 'w_br_a': w_br_a, 'w_br_b': w_br_b, 'w_br_c': w_br_c,
            'w_o': w_o, 'ln1_g': ln1_g, 'ln1_b': ln1_b,
            'w_up': w_up, 'ffn_conv_w': ffn_conv_w, 'ffn_conv_b': ffn_conv_b, 'w_down': w_down,
            'ln2_g': ln2_g, 'ln2_b': ln2_b}


def reference(x_prompt, x_sample, w_in, hy_conv_w, hy_conv_b, hy_filt_w1, hy_filt_b1, hy_filt_w2,
              hy_filt_b2, hy_filt_w3, hy_filt_b3, hy_filt_freq, hy_bias, rg_conv_w, rg_conv_b,
              rg_gate_w, rg_gate_b, rg_lam, w_gate, b_gate, w_br_a, w_br_b, w_br_c, w_o, ln1_g, ln1_b,
              w_up, ffn_conv_w, ffn_conv_b, w_down, ln2_g, ln2_b):
    def layer_params(l):
        return (w_in[l], hy_conv_w[l], hy_conv_b[l], hy_filt_w1[l], hy_filt_b1[l], hy_filt_w2[l],
                hy_filt_b2[l], hy_filt_w3[l], hy_filt_b3[l], hy_filt_freq[l], hy_bias[l],
                rg_conv_w[l], rg_conv_b[l], rg_gate_w[l], rg_gate_b[l], rg_lam[l],
                w_gate[l], b_gate[l], w_br_a[l], w_br_b[l], w_br_c[l], w_o[l], ln1_g[l], ln1_b[l],
                w_up[l], ffn_conv_w[l], ffn_conv_b[l], w_down[l], ln2_g[l], ln2_b[l])

    def trunk(x):
        for l in range(DEPTH):
            x = encoder_layer(x, *layer_params(l))
        return x

    y_prompt = trunk(x_prompt)
    y_sample = trunk(x_sample)
    return (y_prompt, y_sample)
```

```python
import functools
import math

import numpy as np
import jax
import jax.numpy as jnp
from jax import lax
from jax.experimental import pallas as pl
from jax.experimental.pallas import tpu as pltpu

F32 = jnp.float32
BF16 = jnp.bfloat16

D_MODEL = 1024
SEQ = 4096
DEPTH = 2
D_HY = 512
HY_EMB_BANDS = 8
HY_EMB_PAD = 32
HY_FAST_DECAY = 0.3
HY_SLOW_DECAY = 1.5
HY_TARGET = 1e-2
HEAD_DIM = 64
HEADS_PER_GROUP = 4
DILATIONS = (1, 4, 16)
RADIUS = 64
D_GROUP = HEADS_PER_GROUP * HEAD_DIM
D_ATT = len(DILATIONS) * D_GROUP
ROPE_THETA = 10000.0
NEG_INF = -1e30
DEN_FLOOR = 1e-30
D_RG = 512
RG_BLOCKS = 8
RG_C = 8.0
D_FF = 3 * D_MODEL
DN_ALPHA = (2 * DEPTH) ** 0.25
LN_EPS = 1e-5
D_IN = 3 * D_HY + 3 * D_ATT + 2 * D_RG
OFF_Q = 3 * D_HY
OFF_K = OFF_Q + D_ATT
OFF_V = OFF_K + D_ATT
OFF_RG = OFF_V + D_ATT
OFF_RG_GATE = OFF_RG + D_RG

LANES = 128
SUBLANES = 8
VMEM_LIMIT = 56 * 1024 * 1024

FFT_N = 2 * SEQ
FFT_N1 = 64
FFT_N2 = 128
HY_COLS = FFT_N2 * D_HY


def _params(*sem):
    return pltpu.CompilerParams(dimension_semantics=sem, vmem_limit_bytes=VMEM_LIMIT)


def _dot(a, b):
    return jnp.dot(a, b, preferred_element_type=F32)


@functools.lru_cache(maxsize=None)
def _dft_constants():
    n1h = np.arange(FFT_N1 // 2)
    n1 = np.arange(FFT_N1)
    k1 = np.arange(FFT_N1)

    def cs(a, b, period):
        ang = 2.0 * np.pi * (np.outer(a, b) % period) / period
        return np.cos(ang), np.sin(ang)

    c, s = cs(k1, n1h, FFT_N1)
    f1_half = np.concatenate([c, -s], axis=0)
    c, s = cs(k1, n1, FFT_N1)
    f1_full = np.concatenate([c, -s], axis=0)
    c, s = cs(n1h, k1, FFT_N1)
    f3 = np.concatenate([c, -s], axis=1) / FFT_N
    n2 = np.arange(FFT_N2)
    k2 = np.arange(FFT_N2)
    g = np.zeros((FFT_N1, 2 * FFT_N2, 2 * FFT_N2), np.float64)
    for a in range(FFT_N1):
        c, s = cs(a + FFT_N1 * k2, n2, FFT_N)
        gr, gi = c, -s
        g[a] = np.block([[gr, -gi], [gi, gr]])
    return (jnp.asarray(f1_half, BF16), jnp.asarray(f1_full, BF16), jnp.asarray(f3, BF16),
            jnp.asarray(g, BF16))


@functools.lru_cache(maxsize=None)
def _filter_positions():
    L = SEQ
    t = np.linspace(0.0, 1.0, L, dtype=np.float32).astype(np.float64)[:, None]
    w = (2.0 * np.pi * np.arange(L, dtype=np.float32) / L).astype(np.float64)[:, None]
    bands = np.linspace(1e-4, HY_EMB_BANDS - 1, HY_EMB_BANDS, dtype=np.float32).astype(np.float64)[None, :]
    z = np.concatenate([t, np.cos(bands * w), -np.sin(bands * w)], axis=-1)
    m = np.arange(2 * L)
    src = np.where(m < L, m, 2 * L - m)
    src = np.where(m == L, 0, src)
    zp = np.zeros((2 * L, HY_EMB_PAD))
    zp[:, :z.shape[1]] = z[src]
    return jnp.asarray(zp, F32)


@functools.lru_cache(maxsize=None)
def _rope_tables():
    half = HEAD_DIM // 2
    inv = ROPE_THETA ** (-np.arange(0, HEAD_DIM, 2, dtype=np.float32).astype(np.float64) / HEAD_DIM)
    ang = np.arange(SEQ, dtype=np.float64)[:, None] * inv[None, :]
    cos = np.concatenate([np.cos(ang), np.cos(ang)], axis=1)
    sin = np.concatenate([-np.sin(ang), np.sin(ang)], axis=1)
    cos = np.tile(cos, (1, HEADS_PER_GROUP))
    sin = np.tile(sin, (1, HEADS_PER_GROUP))
    assert half * 2 == HEAD_DIM
    return jnp.asarray(cos, F32), jnp.asarray(sin, F32)


PROJ_TM = 512
_PROJ_CHUNKS = tuple((j, min(j + 512, D_IN)) for j in range(0, D_IN, 512))


def _proj_kernel(x_ref, w_ref, o_ref):
    xb = x_ref[...].astype(BF16)
    for lo, hi in _PROJ_CHUNKS:
        o_ref[:, lo:hi] = _dot(xb, w_ref[:, lo:hi])


def _input_projection(x2d, w_in_bf16):
    t = x2d.shape[0]
    return pl.pallas_call(
        _proj_kernel,
        grid=(t // PROJ_TM,),
        in_specs=[pl.BlockSpec((PROJ_TM, D_MODEL), lambda i: (i, 0)),
                  pl.BlockSpec((D_MODEL, D_IN), lambda i: (0, 0), pipeline_mode=pl.Buffered(1))],
        out_specs=pl.BlockSpec((PROJ_TM, D_IN), lambda i: (i, 0)),
        out_shape=jax.ShapeDtypeStruct((t, D_IN), F32),
        compiler_params=_params("parallel"),
        name="input_projection",
    )(x2d, w_in_bf16)


def _conv_rows(x, w, bias, pad_left):
    n = x.shape[0]
    row = lax.broadcasted_iota(jnp.int32, x.shape, 0)
    acc = None
    for k in range(w.shape[0]):
        off = k - pad_left
        if off == 0:
            xs = x
        else:
            xs = pltpu.roll(x, (-off) % n, axis=0)
            valid = (row >= -off) if off < 0 else (row < n - off)
            xs = jnp.where(valid, xs, 0.0)
        term = xs * w[k:k + 1, :]
        acc = term if acc is None else acc + term
    return acc + bias


def _hy_pre_kernel(x0_ref, x1_ref, v_ref, w_ref, b_ref, x0c_ref, uu_ref):
    x0 = _conv_rows(x0_ref[0], w_ref[:, 0, :], b_ref[0], 1)
    x1 = _conv_rows(x1_ref[0], w_ref[:, 1, :], b_ref[1], 1)
    v = _conv_rows(v_ref[0], w_ref[:, 2, :], b_ref[2], 1)
    x0c_ref[0] = x0
    uu_ref[0] = v * x1


def _hyena_pre(proj3, conv_w, conv_b):
    b = proj3.shape[0]
    nchunk = D_HY // LANES
    w = conv_w.reshape(3, 3, D_HY)
    bias = conv_b.reshape(3, 1, D_HY)
    blk = lambda part: pl.BlockSpec((1, SEQ, LANES), lambda i, c, part=part: (i, 0, part * nchunk + c))
    out = pl.BlockSpec((1, SEQ, LANES), lambda i, c: (i, 0, c))
    return pl.pallas_call(
        _hy_pre_kernel,
        grid=(b, nchunk),
        in_specs=[blk(0), blk(1), blk(2),
                  pl.BlockSpec((3, 3, LANES), lambda i, c: (0, 0, c)),
                  pl.BlockSpec((3, 1, LANES), lambda i, c: (0, 0, c))],
        out_specs=[out, out],
        out_shape=[jax.ShapeDtypeStruct((b, SEQ, D_HY), F32)] * 2,
        compiler_params=_params("parallel", "parallel"),
        name="hyena_short_conv",
    )(proj3, proj3, proj3, w, bias)


FILT_ROWS = 512


def _hy_filter_kernel(z_ref, w1_ref, b1_ref, w2_ref, b2_ref, w3f_ref, w3b_ref, b3f_ref, b3b_ref,
                      fr_ref, dl_ref, k_ref):
    hi = lax.Precision.HIGHEST
    fr = fr_ref[0]
    w1, b1, w2, b2 = w1_ref[0], b1_ref[0], w2_ref[0], b2_ref[0]
    w3f, w3b, b3f, b3b = w3f_ref[0], w3b_ref[0], b3f_ref[0], b3b_ref[0]
    delta = dl_ref[...]

    def body(i, total):
        r0 = pl.multiple_of(i * FILT_ROWS, FILT_ROWS)
        z = z_ref[pl.ds(r0, FILT_ROWS), :]
        h = jnp.sin(fr * (jnp.dot(z, w1, precision=hi, preferred_element_type=F32) + b1))
        h = jnp.sin(fr * (jnp.dot(h, w2, precision=hi, preferred_element_type=F32) + b2))
        hf = jnp.dot(h, w3f, precision=hi, preferred_element_type=F32) + b3f
        hb = jnp.dot(h, w3b, precision=hi, preferred_element_type=F32) + b3b
        m = lax.broadcasted_iota(jnp.int32, hf.shape, 0) + r0
        decay = jnp.exp(-z[:, 0:1] * delta)
        val = jnp.where(m < SEQ, hf, hb) * decay
        val = jnp.where(m == SEQ, 0.0, val)
        k_ref[0, pl.ds(r0, FILT_ROWS), :] = val
        return total + jnp.sum(jnp.abs(val), axis=0, keepdims=True)

    total = lax.fori_loop(0, FFT_N // FILT_ROWS, body, jnp.zeros((1, LANES), F32))

    def scale(i, carry):
        r0 = pl.multiple_of(i * FILT_ROWS, FILT_ROWS)
        k_ref[0, pl.ds(r0, FILT_ROWS), :] = k_ref[0, pl.ds(r0, FILT_ROWS), :] / total
        return carry

    lax.fori_loop(0, FFT_N // FILT_ROWS, scale, 0)


def _hyena_filters(w1, b1, w2, b2, w3, b3, freq):
    nchunk = D_HY // LANES
    z = _filter_positions()
    emb = z.shape[1]
    w1 = jnp.pad(w1, ((0, 0), (0, emb - w1.shape[1]), (0, 0)))
    hid = w1.shape[-1]
    deltas = np.abs(np.linspace(math.log(HY_TARGET) / HY_SLOW_DECAY, math.log(HY_TARGET) / HY_FAST_DECAY,
                                D_HY, dtype=np.float32))
    deltas = jnp.asarray(deltas, F32).reshape(1, D_HY)
    row = lambda a: a.reshape(DEPTH, 1, a.shape[-1])
    full = lambda s: pl.BlockSpec((1,) + s, lambda l, c: (l, 0, 0))
    return pl.pallas_call(
        _hy_filter_kernel,
        grid=(DEPTH, nchunk),
        in_specs=[pl.BlockSpec((FFT_N, emb), lambda l, c: (0, 0)),
                  full((emb, hid)), full((1, hid)), full((hid, hid)), full((1, hid)),
                  pl.BlockSpec((1, hid, LANES), lambda l, c: (l, 0, c)),
                  pl.BlockSpec((1, hid, LANES), lambda l, c: (l, 0, nchunk + c)),
                  pl.BlockSpec((1, 1, LANES), lambda l, c: (l, 0, c)),
                  pl.BlockSpec((1, 1, LANES), lambda l, c: (l, 0, nchunk + c)),
                  full((1, hid)),
                  pl.BlockSpec((1, LANES), lambda l, c: (0, c))],
        out_specs=pl.BlockSpec((1, FFT_N, LANES), lambda l, c: (l, 0, c)),
        out_shape=jax.ShapeDtypeStruct((DEPTH, FFT_N, D_HY), F32),
        compiler_params=_params("parallel", "parallel"),
        name="hyena_filter",
    )(z, w1, row(b1), w2, row(b2), w3, w3, row(b3), row(b3), row(freq), deltas)


DFT_TN = 8192


def _dft_rows_kernel(f_ref, x_ref, o_ref):
    o_ref[0] = _dot(f_ref[...], x_ref[0].astype(BF16))


def _dft_over_n1(x_view, f):
    b, k, cols = x_view.shape
    m = f.shape[0]
    return pl.pallas_call(
        _dft_rows_kernel,
        grid=(b, cols // DFT_TN),
        in_specs=[pl.BlockSpec((m, k), lambda i, j: (0, 0)),
                  pl.BlockSpec((1, k, DFT_TN), lambda i, j: (i, 0, j))],
        out_specs=pl.BlockSpec((1, m, DFT_TN), lambda i, j: (i, 0, j)),
        out_shape=jax.ShapeDtypeStruct((b, m, cols), F32),
        compiler_params=_params("parallel", "parallel"),
        name="hyena_dft_n1",
    )(f, x_view)


def _filter_spectrum_kernel(g_ref, a_ref, o_ref):
    a = a_ref[0].reshape(2 * FFT_N2, D_HY).astype(BF16)
    o_ref[0] = _dot(g_ref[0], a).reshape(2, FFT_N2, D_HY)


def _filter_spectrum(a_k, g):
    slab = pl.BlockSpec((1, 2, FFT_N2, D_HY), lambda k1, l: (l, 0, k1, 0))
    return pl.pallas_call(
        _filter_spectrum_kernel,
        grid=(FFT_N1, DEPTH),
        in_specs=[pl.BlockSpec((1, 2 * FFT_N2, 2 * FFT_N2), lambda k1, l: (k1, 0, 0)), slab],
        out_specs=slab,
        out_shape=jax.ShapeDtypeStruct(a_k.shape, F32),
        compiler_params=_params("parallel", "parallel"),
        name="hyena_filter_spectrum",
    )(g, a_k)


def _hy_freq_kernel(g_ref, kh_ref, a_ref, d_ref):
    g = g_ref[0]
    a = a_ref[0].reshape(2 * FFT_N2, D_HY).astype(BF16)
    x = _dot(g, a)
    xr, xi = x[:FFT_N2], x[FFT_N2:]
    kr, ki = kh_ref[0], kh_ref[1]
    zr = xr * kr - xi * ki
    zi = xr * ki + xi * kr
    z = jnp.concatenate([zr, zi], axis=0).astype(BF16)
    d = lax.dot_general(g, z, (((0,), (0,)), ((), ())), preferred_element_type=F32)
    d_ref[0] = d.reshape(2, FFT_N2, D_HY)


def _hyena_freq(a, khat, g):
    b = a.shape[0]
    slab = pl.BlockSpec((1, 2, FFT_N2, D_HY), lambda k1, i: (i, 0, k1, 0))
    return pl.pallas_call(
        _hy_freq_kernel,
        grid=(FFT_N1, b),
        in_specs=[pl.BlockSpec((1, 2 * FFT_N2, 2 * FFT_N2), lambda k1, i: (k1, 0, 0)),
                  pl.BlockSpec((2, FFT_N2, D_HY), lambda k1, i: (0, k1, 0)),
                  slab],
        out_specs=slab,
        out_shape=jax.ShapeDtypeStruct(a.shape, F32),
        compiler_params=_params("parallel", "parallel"),
        name="hyena_freq",
    )(g, khat, a)


def _hy_out_kernel(f_ref, d_ref, uu_ref, x0_ref, bias_ref, o_ref):
    y = _dot(f_ref[...], d_ref[0].astype(BF16))
    o_ref[0] = x0_ref[0] * (y + uu_ref[0] * bias_ref[...])


def _hyena_out(d_view, uu_view, x0_view, bias_tiled, f3):
    b = d_view.shape[0]
    half = FFT_N1 // 2
    row = pl.BlockSpec((1, half, DFT_TN), lambda i, j: (i, 0, j))
    return pl.pallas_call(
        _hy_out_kernel,
        grid=(b, HY_COLS // DFT_TN),
        in_specs=[pl.BlockSpec((half, 2 * FFT_N1), lambda i, j: (0, 0)),
                  pl.BlockSpec((1, 2 * FFT_N1, DFT_TN), lambda i, j: (i, 0, j)),
                  row, row,
                  pl.BlockSpec((1, DFT_TN), lambda i, j: (0, 0))],
        out_specs=row,
        out_shape=jax.ShapeDtypeStruct((b, half, HY_COLS), F32),
        compiler_params=_params("parallel", "parallel"),
        name="hyena_out",
    )(f3, d_view, uu_view, x0_view, bias_tiled)


def _hyena_mixer(proj3, conv_w, conv_b, khat, hy_bias):
    b = proj3.shape[0]
    f1_half, _, f3, g = _dft_constants()
    x0c, uu = _hyena_pre(proj3, conv_w, conv_b)
    half = FFT_N1 // 2
    a = _dft_over_n1(uu.reshape(b, half, HY_COLS), f1_half)
    d = _hyena_freq(a.reshape(b, 2, FFT_N, D_HY), khat, g)
    bias_tiled = jnp.tile(hy_bias.reshape(1, D_HY), (1, DFT_TN // D_HY))
    ya = _hyena_out(d.reshape(b, 2 * FFT_N1, HY_COLS), uu.reshape(b, half, HY_COLS),
                    x0c.reshape(b, half, HY_COLS), bias_tiled, f3)
    return ya.reshape(b, SEQ, D_HY)


def _hyena_filter_spectra(w1, b1, w2, b2, w3, b3, freq):
    _, f1_full, _, g = _dft_constants()
    k = _hyena_filters(w1, b1, w2, b2, w3, b3, freq)
    a_k = _dft_over_n1(k.reshape(DEPTH, FFT_N1, HY_COLS), f1_full)
    return _filter_spectrum(a_k.reshape(DEPTH, 2, FFT_N, D_HY), g)


ATT_TQ = 128
ATT_WIN = ATT_TQ + 2 * RADIUS
ATT_QC = 1024


def _rope(t, cos, sin):
    lane = lax.broadcasted_iota(jnp.int32, t.shape, 1)
    first_half = (lane % HEAD_DIM) < (HEAD_DIM // 2)
    rot = jnp.where(first_half, pltpu.roll(t, D_GROUP - HEAD_DIM // 2, axis=1),
                    pltpu.roll(t, HEAD_DIM // 2, axis=1))
    return t * cos + rot * sin


def _attn_kernel(q_ref, kp_ref, k_ref, kn_ref, vp_ref, v_ref, vn_ref, cp_ref, c_ref, cn_ref,
                 sp_ref, s_ref, sn_ref, o_ref, lse_ref, qs, ks, vs, *, n, qc):
    c0 = pl.program_id(2) * qc
    q = _rope(q_ref[0], c_ref[...], s_ref[...]) * (HEAD_DIM ** -0.5)
    lane = lax.broadcasted_iota(jnp.int32, (qc, D_GROUP), 1)
    even_head = (lane % LANES) < HEAD_DIM
    qs[0] = jnp.where(even_head, q, 0.0).astype(BF16)
    qs[1] = jnp.where(even_head, 0.0, q).astype(BF16)
    ks[0:RADIUS] = _rope(kp_ref[0], cp_ref[...], sp_ref[...]).astype(BF16)
    ks[RADIUS:RADIUS + qc] = _rope(k_ref[0], c_ref[...], s_ref[...]).astype(BF16)
    ks[RADIUS + qc:] = _rope(kn_ref[0], cn_ref[...], sn_ref[...]).astype(BF16)
    vs[0:RADIUS] = vp_ref[0].astype(BF16)
    vs[RADIUS:RADIUS + qc] = v_ref[0].astype(BF16)
    vs[RADIUS + qc:] = vn_ref[0].astype(BF16)

    lane_p = lax.broadcasted_iota(jnp.int32, (ATT_TQ, LANES), 1)
    low_head = lane_p < HEAD_DIM
    row = lax.broadcasted_iota(jnp.int32, (ATT_TQ, ATT_WIN), 0)
    col = lax.broadcasted_iota(jnp.int32, (ATT_TQ, ATT_WIN), 1)
    band = jnp.abs(col - RADIUS - row) - RADIUS

    def body(i, carry):
        q0 = pl.multiple_of(i * ATT_TQ, ATT_TQ)
        pos = col + (c0 + q0 - RADIUS)
        mask = jnp.maximum(jnp.maximum(band, -pos), pos - (n - 1)) <= 0
        for hp in range(D_GROUP // LANES):
            ls = slice(hp * LANES, (hp + 1) * LANES)
            kp = ks[pl.ds(q0, ATT_WIN), ls]
            vp = vs[pl.ds(q0, ATT_WIN), ls]
            outs, lses = [], []
            for h in range(2):
                s = lax.dot_general(qs[h, pl.ds(q0, ATT_TQ), ls], kp, (((1,), (1,)), ((), ())), preferred_element_type=F32)
                s = jnp.where(mask, s, NEG_INF)
                m = jnp.max(s, axis=-1, keepdims=True)
                e = jnp.where(mask, jnp.exp(s - m), 0.0)
                den = jnp.maximum(jnp.sum(e, axis=-1, keepdims=True), DEN_FLOOR)
                outs.append(_dot(e.astype(BF16), vp) / den)
                lses.append(m + jnp.log(den))
            o_ref[0, pl.ds(q0, ATT_TQ), ls] = jnp.where(low_head, outs[0], outs[1])
            lse_ref[0, pl.ds(q0, ATT_TQ), ls] = jnp.where(low_head, lses[0], lses[1])
        return carry

    lax.fori_loop(0, qc // ATT_TQ, body, 0)


def _dilated_attention_group(proj3, g, dil):
    b = proj3.shape[0]
    n = SEQ // dil
    qc = min(n, ATT_QC)
    hpc = qc // RADIUS
    last = n // RADIUS - 1
    cos, sin = _rope_tables()
    pv = proj3.reshape(b, n, dil * D_IN)
    cpb = D_IN // D_GROUP

    def specs(col_of, lead):
        return [pl.BlockSpec(lead + (RADIUS, D_GROUP),
                             lambda i, r, c: (i,) * len(lead) + (jnp.maximum(c * hpc - 1, 0), col_of(r))),
                pl.BlockSpec(lead + (qc, D_GROUP), lambda i, r, c: (i,) * len(lead) + (c, col_of(r))),
                pl.BlockSpec(lead + (RADIUS, D_GROUP),
                             lambda i, r, c: (i,) * len(lead) + (jnp.minimum((c + 1) * hpc, last), col_of(r)))]

    proj_col = lambda off: (lambda r: r * cpb + off // D_GROUP + g)
    q_spec = specs(proj_col(OFF_Q), (1,))[1]
    out = pl.BlockSpec((1, qc, D_GROUP), lambda i, r, c: (i, c, r))
    tables = specs(lambda r: r, ())
    cos_v = cos.reshape(n, dil * D_GROUP)
    sin_v = sin.reshape(n, dil * D_GROUP)
    o, lse = pl.pallas_call(
        functools.partial(_attn_kernel, n=n, qc=qc),
        grid=(b, dil, n // qc),
        in_specs=[q_spec] + specs(proj_col(OFF_K), (1,)) + specs(proj_col(OFF_V), (1,)) + tables + tables,
        out_specs=[out, out],
        out_shape=[jax.ShapeDtypeStruct((b, n, dil * D_GROUP), F32)] * 2,
        scratch_shapes=[pltpu.VMEM((2, qc, D_GROUP), BF16), pltpu.VMEM((qc + 2 * RADIUS, D_GROUP), BF16),
                        pltpu.VMEM((qc + 2 * RADIUS, D_GROUP), BF16)],
        compiler_params=_params("parallel", "parallel", "parallel"),
        name=f"dilated_attention_d{dil}",
    )(pv, pv, pv, pv, pv, pv, pv, cos_v, cos_v, cos_v, sin_v, sin_v, sin_v)
    return o.reshape(b * SEQ, D_GROUP), lse.reshape(b * SEQ, D_GROUP)


RG_ROWS = 512
RG_SEG = SEQ // SUBLANES


EXPM1_SERIES_BELOW = 0.125
EXPM1_SERIES_TERMS = 7


def _one_minus_exp(y, exp_y):
    ys = jnp.maximum(y, -EXPM1_SERIES_BELOW)
    poly = 1.0 + ys * (1.0 / EXPM1_SERIES_TERMS)
    for k in range(EXPM1_SERIES_TERMS - 1, 1, -1):
        poly = 1.0 + ys * (1.0 / k) * poly
    return jnp.where(y > -EXPM1_SERIES_BELOW, -ys * poly, 1.0 - exp_y)


def _rglru_kernel(xr_ref, gate_ref, cw_ref, cb_ref, wg_ref, bg_ref, lam_ref, o_ref, a_s, x_s, h_s, p_s):
    x_s[0] = _conv_rows(xr_ref[0], cw_ref[...], cb_ref[...], 2)
    lam = lam_ref[...]
    softplus = jnp.maximum(-lam, 0.0) + jnp.log1p(jnp.exp(-jnp.abs(lam)))
    wg = wg_ref[0]
    bg = bg_ref[0]

    def gates(i, carry):
        r0 = pl.multiple_of(i * RG_ROWS, RG_ROWS)
        xr = x_s[0, pl.ds(r0, RG_ROWS), :]
        g = _dot(xr.astype(BF16), wg) + bg
        row = lax.broadcasted_iota(jnp.int32, xr.shape, 0) + r0
        for d in range(2):
            r = jax.nn.sigmoid(g[:, (2 * d) * LANES:(2 * d + 1) * LANES])
            ig = jax.nn.sigmoid(g[:, (2 * d + 1) * LANES:(2 * d + 2) * LANES])
            log_a = -RG_C * r * softplus[d:d + 1, :]
            a = jnp.exp(log_a)
            mult = jnp.sqrt(_one_minus_exp(2.0 * log_a, a * a))
            start = 0 if d == 0 else SEQ - 1
            mult = jnp.where(row == start, 1.0, mult)
            a_s[d, pl.ds(r0, RG_ROWS), :] = a
            p_s[d, pl.ds(r0, RG_ROWS), :] = xr * ig * mult
        return carry

    lax.fori_loop(0, SEQ // RG_ROWS, gates, 0)

    def scan(j, carry):
        hf, pf, hb, pb = carry
        jb = RG_SEG - 1 - j
        idx_f = pl.ds(j, SUBLANES, stride=RG_SEG)
        idx_b = pl.ds(jb, SUBLANES, stride=RG_SEG)
        af = a_s[0, idx_f, :]
        hf = af * hf + p_s[0, idx_f, :]
        pf = pf * af
        h_s[0, idx_f, :] = hf
        x_s[0, idx_f, :] = pf
        ab = a_s[1, idx_b, :]
        hb = ab * hb + p_s[1, idx_b, :]
        pb = pb * ab
        h_s[1, idx_b, :] = hb
        x_s[1, idx_b, :] = pb
        return hf, pf, hb, pb

    zero = jnp.zeros((SUBLANES, LANES), F32)
    one = jnp.ones((SUBLANES, LANES), F32)
    hf, pf, hb, pb = lax.fori_loop(0, RG_SEG, scan, (zero, one, zero, one), unroll=8)

    sub = lax.broadcasted_iota(jnp.int32, (SUBLANES, LANES), 0)
    cf = zero
    for s in range(1, SUBLANES):
        cf = jnp.where(sub == s, pltpu.roll(hf + pf * cf, 1, axis=0), cf)
    cb = zero
    for s in range(SUBLANES - 2, -1, -1):
        cb = jnp.where(sub == s, pltpu.roll(hb + pb * cb, SUBLANES - 1, axis=0), cb)

    for s in range(SUBLANES):
        rows = pl.ds(s * RG_SEG, RG_SEG)
        h = (h_s[0, rows, :] + x_s[0, rows, :] * cf[s:s + 1, :]
             + h_s[1, rows, :] + x_s[1, rows, :] * cb[s:s + 1, :])
        o_ref[0, rows, :] = h * jax.nn.gelu(gate_ref[0, rows, :])


def _rglru_gate_weights(gate_w, gate_b):
    nchunk = D_RG // LANES
    bd = D_RG // RG_BLOCKS
    per = LANES // bd
    w = jnp.zeros((nchunk, LANES, 4, LANES), F32)
    for c in range(nchunk):
        for j in range(per):
            blk = gate_w[:, :, c * per + j].reshape(4, bd, bd)
            w = w.at[c, j * bd:(j + 1) * bd, :, j * bd:(j + 1) * bd].set(jnp.transpose(blk, (1, 0, 2)))
    w = w.reshape(nchunk, LANES, 4 * LANES).astype(BF16)
    b = gate_b.reshape(4, nchunk, LANES).transpose(1, 0, 2).reshape(nchunk, 1, 4 * LANES)
    return w, b


def _rglru_mixer(proj3, conv_w, conv_b, gate_w, gate_b, lam):
    b = proj3.shape[0]
    nchunk = D_RG // LANES
    wg, bg = _rglru_gate_weights(gate_w, gate_b)
    blk = lambda off: pl.BlockSpec((1, SEQ, LANES), lambda i, c, off=off: (i, 0, off // LANES + c))
    return pl.pallas_call(
        _rglru_kernel,
        grid=(b, nchunk),
        in_specs=[blk(OFF_RG), blk(OFF_RG_GATE),
                  pl.BlockSpec((conv_w.shape[0], LANES), lambda i, c: (0, c)),
                  pl.BlockSpec((1, LANES), lambda i, c: (0, c)),
                  pl.BlockSpec((1, LANES, 4 * LANES), lambda i, c: (c, 0, 0)),
                  pl.BlockSpec((1, 1, 4 * LANES), lambda i, c: (c, 0, 0)),
                  pl.BlockSpec((2, LANES), lambda i, c: (0, c))],
        out_specs=pl.BlockSpec((1, SEQ, LANES), lambda i, c: (i, 0, c)),
        out_shape=jax.ShapeDtypeStruct((b, SEQ, D_RG), F32),
        scratch_shapes=[pltpu.VMEM((2, SEQ, LANES), F32)] * 4,
        compiler_params=_params("parallel", "parallel"),
        name="rglru",
    )(proj3, proj3, conv_w, conv_b.reshape(1, D_RG), wg, bg, lam)


MERGE_TM = 256


def _layer_norm(z, g, b):
    mu = jnp.mean(z, axis=-1, keepdims=True)
    zc = z - mu
    var = jnp.mean(zc * zc, axis=-1, keepdims=True)
    return zc * lax.rsqrt(var + LN_EPS) * g + b


def _merge_kernel(x_ref, ya_ref, o0_ref, o1_ref, o2_ref, l0_ref, l1_ref, l2_ref, yc_ref,
                  wg_ref, bgate_ref, wa_ref, wb_ref, wc_ref, wo_ref, g_ref, b_ref, out_ref):
    x = x_ref[...]
    xb = x.astype(BF16)
    l0, l1, l2 = l0_ref[...], l1_ref[...], l2_ref[...]
    lmax = jnp.maximum(jnp.maximum(l0, l1), l2)
    e0, e1, e2 = jnp.exp(l0 - lmax), jnp.exp(l1 - lmax), jnp.exp(l2 - lmax)
    yb = (e0 * o0_ref[...] + e1 * o1_ref[...] + e2 * o2_ref[...]) / (e0 + e1 + e2)
    branches = ((ya_ref[...], wa_ref), (yb, wb_ref), (yc_ref[...], wc_ref))
    mixed = None
    for j, (y, w_ref) in enumerate(branches):
        cols = slice(j * D_MODEL, (j + 1) * D_MODEL)
        gate = jax.nn.sigmoid(_dot(xb, wg_ref[:, cols]) + bgate_ref[:, cols])
        term = gate * _dot(y.astype(BF16), w_ref[...])
        mixed = term if mixed is None else mixed + term
    z = DN_ALPHA * x + _dot(mixed.astype(BF16), wo_ref[...])
    out_ref[...] = _layer_norm(z, g_ref[...], b_ref[...])


def _merge_and_project(x2d, ya, att, yc, w_gate, b_gate, w_a, w_b, w_c, w_o, ln_g, ln_b):
    t = x2d.shape[0]
    rows = lambda w: pl.BlockSpec((MERGE_TM, w), lambda i: (i, 0))
    const = lambda a: pl.BlockSpec(a.shape, lambda i: (0,) * a.ndim, pipeline_mode=pl.Buffered(1))
    (o0, l0), (o1, l1), (o2, l2) = att
    weights = (w_gate, b_gate.reshape(1, -1), w_a, w_b, w_c, w_o, ln_g.reshape(1, -1), ln_b.reshape(1, -1))
    return pl.pallas_call(
        _merge_kernel,
        grid=(t // MERGE_TM,),
        in_specs=[rows(D_MODEL), rows(D_HY)] + [rows(D_GROUP)] * 6 + [rows(D_RG)]
                 + [const(a) for a in weights],
        out_specs=rows(D_MODEL),
        out_shape=jax.ShapeDtypeStruct((t, D_MODEL), F32),
        compiler_params=_params("parallel"),
        name="merge_project_norm",
    )(x2d, ya, o0, o1, o2, l0, l1, l2, yc, *weights)


FFN_TM = 512
FFN_CF = 512


def _ffn_kernel(x_ref, prev_ref, next_ref, wup_ref, cw_ref, cb_ref, wdn_ref, g_ref, b_ref, out_ref):
    i = pl.program_id(0)
    tiles_per_seq = SEQ // FFN_TM
    x = x_ref[...]
    xb = x.astype(BF16)
    has_prev = (i % tiles_per_seq) != 0
    has_next = (i % tiles_per_seq) != tiles_per_seq - 1
    halo = jnp.concatenate([prev_ref[...], next_ref[...]], axis=0).astype(BF16)
    row = lax.broadcasted_iota(jnp.int32, (FFN_TM, FFN_CF), 0)
    acc = None
    for c in range(D_FF // FFN_CF):
        cols = slice(c * FFN_CF, (c + 1) * FFN_CF)
        ucols = slice(D_FF + c * FFN_CF, D_FF + (c + 1) * FFN_CF)
        hg = _dot(xb, wup_ref[:, cols])
        hh = _dot(halo, wup_ref[:, cols])
        before = jnp.where(has_prev, hh[SUBLANES - 1:SUBLANES, :], 0.0)
        after = jnp.where(has_next, hh[SUBLANES:SUBLANES + 1, :], 0.0)
        up = jnp.where(row == 0, before, pltpu.roll(hg, 1, axis=0))
        dn = jnp.where(row == FFN_TM - 1, after, pltpu.roll(hg, FFN_TM - 1, axis=0))
        w = cw_ref[:, cols]
        conv = up * w[0:1, :] + hg * w[1:2, :] + dn * w[2:3, :] + cb_ref[:, cols]
        act = jax.nn.gelu(conv) * _dot(xb, wup_ref[:, ucols])
        term = _dot(act.astype(BF16), wdn_ref[cols, :])
        acc = term if acc is None else acc + term
    out_ref[...] = _layer_norm(DN_ALPHA * x + acc, g_ref[...], b_ref[...])


def _ffn(x2d, w_up, conv_w, conv_b, w_down, ln_g, ln_b):
    t = x2d.shape[0]
    bpt = FFN_TM // SUBLANES
    nblk = t // SUBLANES
    const = lambda a: pl.BlockSpec(a.shape, lambda i: (0,) * a.ndim, pipeline_mode=pl.Buffered(1))
    weights = (w_up, conv_w, conv_b.reshape(1, -1), w_down, ln_g.reshape(1, -1), ln_b.reshape(1, -1))
    return pl.pallas_call(
        _ffn_kernel,
        grid=(t // FFN_TM,),
        in_specs=[pl.BlockSpec((FFN_TM, D_MODEL), lambda i: (i, 0)),
                  pl.BlockSpec((SUBLANES, D_MODEL), lambda i: (jnp.maximum(i * bpt - 1, 0), 0)),
                  pl.BlockSpec((SUBLANES, D_MODEL), lambda i: (jnp.minimum((i + 1) * bpt, nblk - 1), 0))]
                 + [const(a) for a in weights],
        out_specs=pl.BlockSpec((FFN_TM, D_MODEL), lambda i: (i, 0)),
        out_shape=jax.ShapeDtypeStruct((t, D_MODEL), F32),
        compiler_params=_params("parallel"),
        name="ffn_norm",
    )(x2d, x2d, x2d, *weights)


def _encoder_layer(x2d, b, l, p, khat):
    proj = _input_projection(x2d, p["w_in"][l])
    proj3 = proj.reshape(b, SEQ, D_IN)
    ya = _hyena_mixer(proj3, p["hy_conv_w"][l], p["hy_conv_b"][l], khat[l], p["hy_bias"][l])
    att = [_dilated_attention_group(proj3, g, dil) for g, dil in enumerate(DILATIONS)]
    yc = _rglru_mixer(proj3, p["rg_conv_w"][l], p["rg_conv_b"][l], p["rg_gate_w"][l], p["rg_gate_b"][l],
                      p["rg_lam"][l])
    x2d = _merge_and_project(x2d, ya.reshape(b * SEQ, D_HY), att, yc.reshape(b * SEQ, D_RG),
                             p["w_gate"][l], p["b_gate"][l], p["w_br_a"][l], p["w_br_b"][l], p["w_br_c"][l],
                             p["w_o"][l], p["ln1_g"][l], p["ln1_b"][l])
    return _ffn(x2d, p["w_up"][l], p["ffn_conv_w"][l], p["ffn_conv_b"][l], p["w_down"][l],
                p["ln2_g"][l], p["ln2_b"][l])


def _trunk(x, p):
    b = x.shape[0]
    khat = _hyena_filter_spectra(p["hy_filt_w1"], p["hy_filt_b1"], p["hy_filt_w2"], p["hy_filt_b2"],
                                 p["hy_filt_w3"], p["hy_filt_b3"], p["hy_filt_freq"])
    x2d = x.reshape(b * SEQ, D_MODEL)
    for l in range(DEPTH):
        x2d = _encoder_layer(x2d, b, l, p, khat)
    return x2d.reshape(b, SEQ, D_MODEL)


_MATMUL_WEIGHTS = ("w_in", "w_gate", "w_br_a", "w_br_b", "w_br_c", "w_o", "w_up", "w_down")


def kernel(x_prompt, x_sample, w_in, hy_conv_w, hy_conv_b, hy_filt_w1, hy_filt_b1, hy_filt_w2, hy_filt_b2, hy_filt_w3, hy_filt_b3, hy_filt_freq, hy_bias, rg_conv_w, rg_conv_b, rg_gate_w, rg_gate_b, rg_lam, w_gate, b_gate, w_br_a, w_br_b, w_br_c, w_o, ln1_g, ln1_b, w_up, ffn_conv_w, ffn_conv_b, w_down, ln2_g, ln2_b):
    p = dict(w_in=w_in, hy_conv_w=hy_conv_w, hy_conv_b=hy_conv_b, hy_filt_w1=hy_filt_w1,
             hy_filt_b1=hy_filt_b1, hy_filt_w2=hy_filt_w2, hy_filt_b2=hy_filt_b2, hy_filt_w3=hy_filt_w3,
             hy_filt_b3=hy_filt_b3, hy_filt_freq=hy_filt_freq, hy_bias=hy_bias, rg_conv_w=rg_conv_w,
             rg_conv_b=rg_conv_b, rg_gate_w=rg_gate_w, rg_gate_b=rg_gate_b, rg_lam=rg_lam, w_gate=w_gate,
             b_gate=b_gate, w_br_a=w_br_a, w_br_b=w_br_b, w_br_c=w_br_c, w_o=w_o, ln1_g=ln1_g, ln1_b=ln1_b,
             w_up=w_up, ffn_conv_w=ffn_conv_w, ffn_conv_b=ffn_conv_b, w_down=w_down, ln2_g=ln2_g,
             ln2_b=ln2_b)
    for name in _MATMUL_WEIGHTS:
        p[name] = p[name].astype(BF16)
    nb = x_prompt.shape[0]
    y = _trunk(jnp.concatenate([x_prompt, x_sample], axis=0), p)
    return (y[:nb], y[nb:])
```

```python
import functools
import math

import numpy as np
import jax
import jax.numpy as jnp
from jax import lax
from jax.experimental import pallas as pl
from jax.experimental.pallas import tpu as pltpu

F32 = jnp.float32
BF16 = jnp.bfloat16

D_MODEL = 1024
SEQ = 4096
DEPTH = 2
D_HY = 512
HY_EMB_BANDS = 8
HY_EMB_PAD = 32
HY_FAST_DECAY = 0.3
HY_SLOW_DECAY = 1.5
HY_TARGET = 1e-2
HEAD_DIM = 64
HEADS_PER_GROUP = 4
DILATIONS = (1, 4, 16)
RADIUS = 64
D_GROUP = HEADS_PER_GROUP * HEAD_DIM
D_ATT = len(DILATIONS) * D_GROUP
ROPE_THETA = 10000.0
NEG_INF = -1e30
DEN_FLOOR = 1e-30
D_RG = 512
RG_BLOCKS = 8
RG_C = 8.0
D_FF = 3 * D_MODEL
DN_ALPHA = (2 * DEPTH) ** 0.25
LN_EPS = 1e-5
D_IN = 3 * D_HY + 3 * D_ATT + 2 * D_RG
OFF_Q = 3 * D_HY
OFF_K = OFF_Q + D_ATT
OFF_V = OFF_K + D_ATT
OFF_RG = OFF_V + D_ATT
D_MAIN = 3 * D_HY + 2 * D_RG
MAIN_RG = 3 * D_HY
MAIN_RG_GATE = MAIN_RG + D_RG

LANES = 128
SUBLANES = 8
VMEM_LIMIT = 56 * 1024 * 1024

FFT_N = 2 * SEQ
FFT_N1 = 64
FFT_N2 = 128
FFT_HALF = FFT_N1 // 2
HY_COLS = FFT_N2 * D_HY
HY_SLABS = FFT_N1 // 2 + 1
HY_SLABS_PAD = 40


def _params(*sem):
    return pltpu.CompilerParams(dimension_semantics=sem, vmem_limit_bytes=VMEM_LIMIT)


def _dot(a, b):
    return jnp.dot(a, b, preferred_element_type=F32)


def _cos_sin(a, b, period):
    ang = 2.0 * np.pi * (np.outer(a, b) % period) / period
    return np.cos(ang), np.sin(ang)


@functools.lru_cache(maxsize=None)
def _dft_constants():
    k1 = np.arange(FFT_N1)
    c, s = _cos_sin(k1, np.arange(FFT_N1), FFT_N1)
    f1_full = np.concatenate([c, -s], axis=0)
    n2 = np.arange(FFT_N2)
    k2 = np.arange(FFT_N2)
    g = np.zeros((FFT_N1, 2 * FFT_N2, 2 * FFT_N2), np.float64)
    for a in range(FFT_N1):
        c, s = _cos_sin(a + FFT_N1 * k2, n2, FFT_N)
        gr, gi = c, -s
        g[a] = np.block([[gr, -gi], [gi, gr]])
    return jnp.asarray(f1_full, BF16), jnp.asarray(g, BF16)


@functools.lru_cache(maxsize=None)
def _hyena_stage_constants():
    k1 = np.arange(HY_SLABS_PAD)
    n1 = np.arange(FFT_HALF)
    eye = np.eye(SUBLANES)
    c, s = _cos_sin(k1, n1, FFT_N1)
    m1 = np.kron(np.concatenate([c, -s], axis=0), eye)
    wgt = np.where((k1 == 0) | (k1 == FFT_N1 // 2), 1.0, 2.0) * (k1 < HY_SLABS)
    c, s = _cos_sin(n1, k1, FFT_N1)
    m3 = np.kron(np.concatenate([c * wgt, -s * wgt], axis=1) / FFT_N, eye)
    return jnp.asarray(m1, BF16), jnp.asarray(m3, BF16)


@functools.lru_cache(maxsize=None)
def _filter_positions():
    L = SEQ
    t = np.linspace(0.0, 1.0, L, dtype=np.float32).astype(np.float64)[:, None]
    w = (2.0 * np.pi * np.arange(L, dtype=np.float32) / L).astype(np.float64)[:, None]
    bands = np.linspace(1e-4, HY_EMB_BANDS - 1, HY_EMB_BANDS, dtype=np.float32).astype(np.float64)[None, :]
    z = np.concatenate([t, np.cos(bands * w), -np.sin(bands * w)], axis=-1)
    m = np.arange(2 * L)
    src = np.where(m < L, m, 2 * L - m)
    src = np.where(m == L, 0, src)
    zp = np.zeros((2 * L, HY_EMB_PAD))
    zp[:, :z.shape[1]] = z[src]
    return jnp.asarray(zp, F32)


@functools.lru_cache(maxsize=None)
def _rope_tables():
    inv = ROPE_THETA ** (-np.arange(0, HEAD_DIM, 2, dtype=np.float32).astype(np.float64) / HEAD_DIM)
    ang = np.arange(SEQ, dtype=np.float64)[:, None] * inv[None, :]
    cos = np.concatenate([np.cos(ang), np.cos(ang)], axis=1)
    sin = np.concatenate([-np.sin(ang), np.sin(ang)], axis=1)
    cos = np.tile(cos, (1, HEADS_PER_GROUP))
    sin = np.tile(sin, (1, HEADS_PER_GROUP))
    return jnp.asarray(cos, F32), jnp.asarray(sin, F32)


PROJ_TM = 512
_MAIN_CHUNKS = tuple((src, dst, 512) for src, dst in
                     [(j, j) for j in range(0, 3 * D_HY, 512)] +
                     [(OFF_RG + j, MAIN_RG + j) for j in range(0, 2 * D_RG, 512)])


def _rope(t, cos, sin):
    lane = lax.broadcasted_iota(jnp.int32, t.shape, 1)
    first_half = (lane % HEAD_DIM) < (HEAD_DIM // 2)
    rot = jnp.where(first_half, pltpu.roll(t, D_GROUP - HEAD_DIM // 2, axis=1),
                    pltpu.roll(t, HEAD_DIM // 2, axis=1))
    return t * cos + rot * sin


def _proj_kernel(x_ref, w_ref, cos_ref, sin_ref, main_ref, a0_ref, a1_ref, a2_ref, qkv_s):
    xb = x_ref[...].astype(BF16)
    for src, dst, width in _MAIN_CHUNKS:
        main_ref[:, dst:dst + width] = _dot(xb, w_ref[:, src:src + width])
    cos = cos_ref[...]
    sin = sin_ref[...]
    for g, (dil, att_ref) in enumerate(zip(DILATIONS, (a0_ref, a1_ref, a2_ref))):
        col = lambda off: slice(off + g * D_GROUP, off + (g + 1) * D_GROUP)
        q = _rope(_dot(xb, w_ref[:, col(OFF_Q)]), cos, sin) * (HEAD_DIM ** -0.5)
        k = _rope(_dot(xb, w_ref[:, col(OFF_K)]), cos, sin)
        v = _dot(xb, w_ref[:, col(OFF_V)])
        if dil == 1:
            att_ref[0, 0, :, 0:D_GROUP] = q
            att_ref[0, 0, :, D_GROUP:2 * D_GROUP] = k
            att_ref[0, 0, :, 2 * D_GROUP:] = v
        else:
            for j, t in enumerate((q, k, v)):
                for h in range(D_GROUP // LANES):
                    qkv_s[j * (D_GROUP // LANES) + h] = t[:, h * LANES:(h + 1) * LANES]
            for r in range(dil):
                for j in range(3 * D_GROUP // LANES):
                    att_ref[0, r, :, j * LANES:(j + 1) * LANES] = qkv_s[j, pl.ds(r, PROJ_TM // dil, stride=dil), :]


def _input_projection(x2d, w_in_bf16):
    t = x2d.shape[0]
    b = t // SEQ
    tps = SEQ // PROJ_TM
    cos, sin = _rope_tables()
    tab = pl.BlockSpec((PROJ_TM, D_GROUP), lambda i: (i % tps, 0))
    att_spec = lambda d: pl.BlockSpec((1, d, PROJ_TM // d, 3 * D_GROUP), lambda i: (i // tps, 0, i % tps, 0))
    att_shape = lambda d: jax.ShapeDtypeStruct((b, d, SEQ // d, 3 * D_GROUP), F32)
    return pl.pallas_call(
        _proj_kernel,
        grid=(t // PROJ_TM,),
        in_specs=[pl.BlockSpec((PROJ_TM, D_MODEL), lambda i: (i, 0)),
                  pl.BlockSpec((D_MODEL, D_IN), lambda i: (0, 0), pipeline_mode=pl.Buffered(1)),
                  tab, tab],
        out_specs=[pl.BlockSpec((PROJ_TM, D_MAIN), lambda i: (i, 0))] + [att_spec(d) for d in DILATIONS],
        out_shape=[jax.ShapeDtypeStruct((t, D_MAIN), F32)] + [att_shape(d) for d in DILATIONS],
        scratch_shapes=[pltpu.VMEM((3 * D_GROUP // LANES, PROJ_TM, LANES), F32)],
        compiler_params=_params("parallel"),
        name="input_projection",
    )(x2d, w_in_bf16, cos, sin)


def _conv_rows(x, w, bias, pad_left):
    n = x.shape[0]
    row = lax.broadcasted_iota(jnp.int32, x.shape, 0)
    acc = None
    for k in range(w.shape[0]):
        off = k - pad_left
        if off == 0:
            xs = x
        else:
            xs = pltpu.roll(x, (-off) % n, axis=0)
            valid = (row >= -off) if off < 0 else (row < n - off)
            xs = jnp.where(valid, xs, 0.0)
        term = xs * w[k:k + 1, :]
        acc = term if acc is None else acc + term
    return acc + bias


FILT_ROWS = 512


def _hy_filter_kernel(z_ref, w1_ref, b1_ref, w2_ref, b2_ref, w3f_ref, w3b_ref, b3f_ref, b3b_ref,
                      fr_ref, dl_ref, k_ref):
    hi = lax.Precision.HIGHEST
    fr = fr_ref[0]
    w1, b1, w2, b2 = w1_ref[0], b1_ref[0], w2_ref[0], b2_ref[0]
    w3f, w3b, b3f, b3b = w3f_ref[0], w3b_ref[0], b3f_ref[0], b3b_ref[0]
    delta = dl_ref[...]

    def body(i, total):
        r0 = pl.multiple_of(i * FILT_ROWS, FILT_ROWS)
        z = z_ref[pl.ds(r0, FILT_ROWS), :]
        h = jnp.sin(fr * (jnp.dot(z, w1, precision=hi, preferred_element_type=F32) + b1))
        h = jnp.sin(fr * (jnp.dot(h, w2, precision=hi, preferred_element_type=F32) + b2))
        hf = jnp.dot(h, w3f, precision=hi, preferred_element_type=F32) + b3f
        hb = jnp.dot(h, w3b, precision=hi, preferred_element_type=F32) + b3b
        m = lax.broadcasted_iota(jnp.int32, hf.shape, 0) + r0
        decay = jnp.exp(-z[:, 0:1] * delta)
        val = jnp.where(m < SEQ, hf, hb) * decay
        val = jnp.where(m == SEQ, 0.0, val)
        k_ref[0, pl.ds(r0, FILT_ROWS), :] = val
        return total + jnp.sum(jnp.abs(val), axis=0, keepdims=True)

    total = lax.fori_loop(0, FFT_N // FILT_ROWS, body, jnp.zeros((1, LANES), F32))

    def scale(i, carry):
        r0 = pl.multiple_of(i * FILT_ROWS, FILT_ROWS)
        k_ref[0, pl.ds(r0, FILT_ROWS), :] = k_ref[0, pl.ds(r0, FILT_ROWS), :] / total
        return carry

    lax.fori_loop(0, FFT_N // FILT_ROWS, scale, 0)


def _hyena_filters(w1, b1, w2, b2, w3, b3, freq):
    nchunk = D_HY // LANES
    z = _filter_positions()
    emb = z.shape[1]
    w1 = jnp.pad(w1, ((0, 0), (0, emb - w1.shape[1]), (0, 0)))
    hid = w1.shape[-1]
    deltas = np.abs(np.linspace(math.log(HY_TARGET) / HY_SLOW_DECAY, math.log(HY_TARGET) / HY_FAST_DECAY,
                                D_HY, dtype=np.float32))
    deltas = jnp.asarray(deltas, F32).reshape(1, D_HY)
    row = lambda a: a.reshape(DEPTH, 1, a.shape[-1])
    full = lambda s: pl.BlockSpec((1,) + s, lambda l, c: (l, 0, 0))
    return pl.pallas_call(
        _hy_filter_kernel,
        grid=(DEPTH, nchunk),
        in_specs=[pl.BlockSpec((FFT_N, emb), lambda l, c: (0, 0)),
                  full((emb, hid)), full((1, hid)), full((hid, hid)), full((1, hid)),
                  pl.BlockSpec((1, hid, LANES), lambda l, c: (l, 0, c)),
                  pl.BlockSpec((1, hid, LANES), lambda l, c: (l, 0, nchunk + c)),
                  pl.BlockSpec((1, 1, LANES), lambda l, c: (l, 0, c)),
                  pl.BlockSpec((1, 1, LANES), lambda l, c: (l, 0, nchunk + c)),
                  full((1, hid)),
                  pl.BlockSpec((1, LANES), lambda l, c: (0, c))],
        out_specs=pl.BlockSpec((1, FFT_N, LANES), lambda l, c: (l, 0, c)),
        out_shape=jax.ShapeDtypeStruct((DEPTH, FFT_N, D_HY), F32),
        compiler_params=_params("parallel", "parallel"),
        name="hyena_filter",
    )(z, w1, row(b1), w2, row(b2), w3, w3, row(b3), row(b3), row(freq), deltas)


DFT_TN = 8192


def _dft_rows_kernel(f_ref, x_ref, o_ref):
    o_ref[0] = _dot(f_ref[...], x_ref[0].astype(BF16))


def _dft_over_n1(x_view, f):
    b, k, cols = x_view.shape
    m = f.shape[0]
    return pl.pallas_call(
        _dft_rows_kernel,
        grid=(b, cols // DFT_TN),
        in_specs=[pl.BlockSpec((m, k), lambda i, j: (0, 0)),
                  pl.BlockSpec((1, k, DFT_TN), lambda i, j: (i, 0, j))],
        out_specs=pl.BlockSpec((1, m, DFT_TN), lambda i, j: (i, 0, j)),
        out_shape=jax.ShapeDtypeStruct((b, m, cols), F32),
        compiler_params=_params("parallel", "parallel"),
        name="hyena_dft_n1",
    )(f, x_view)


def _filter_spectrum_kernel(g_ref, a_ref, o_ref):
    a = a_ref[0].reshape(2 * FFT_N2, D_HY).astype(BF16)
    o_ref[0] = _dot(g_ref[0], a).reshape(2, FFT_N2, D_HY)


def _filter_spectrum(a_k, g):
    slab = pl.BlockSpec((1, 2, FFT_N2, D_HY), lambda k1, l: (l, 0, k1, 0))
    return pl.pallas_call(
        _filter_spectrum_kernel,
        grid=(FFT_N1, DEPTH),
        in_specs=[pl.BlockSpec((1, 2 * FFT_N2, 2 * FFT_N2), lambda k1, l: (k1, 0, 0)), slab],
        out_specs=slab,
        out_shape=jax.ShapeDtypeStruct(a_k.shape, F32),
        compiler_params=_params("parallel", "parallel"),
        name="hyena_filter_spectrum",
    )(g, a_k)


def _hyena_filter_spectra(w1, b1, w2, b2, w3, b3, freq):
    f1_full, g = _dft_constants()
    k = _hyena_filters(w1, b1, w2, b2, w3, b3, freq)
    a_k = _dft_over_n1(k.reshape(DEPTH, FFT_N1, HY_COLS), f1_full)
    return _filter_spectrum(a_k.reshape(DEPTH, 2, FFT_N, D_HY), g)


def _hyena_kernel(x0_ref, x1_ref, v_ref, cw_ref, cb_ref, m1_ref, m3_ref, g_ref, kh_ref, bias_ref, o_ref,
                  uu_s, x0_s, a_s):
    shape3 = (FFT_HALF, FFT_N2, LANES)
    nat = lambda ref: ref[0].reshape(SEQ, LANES)
    x0_s[...] = _conv_rows(nat(x0_ref), cw_ref[:, 0, :], cb_ref[0], 1).reshape(shape3)
    x1 = _conv_rows(nat(x1_ref), cw_ref[:, 1, :], cb_ref[1], 1)
    v = _conv_rows(nat(v_ref), cw_ref[:, 2, :], cb_ref[2], 1)
    uu_s[...] = (v * x1).reshape(shape3)

    def stage1(j, carry):
        n2 = pl.ds(pl.multiple_of(j * SUBLANES, SUBLANES), SUBLANES)
        xb = uu_s[:, n2, :].reshape(FFT_HALF * SUBLANES, LANES).astype(BF16)
        a = _dot(m1_ref[...], xb).reshape(2 * HY_SLABS_PAD, SUBLANES, LANES)
        a_s[0, :, n2, :] = a[:HY_SLABS_PAD]
        a_s[1, :, n2, :] = a[HY_SLABS_PAD:]
        return carry

    lax.fori_loop(0, FFT_N2 // SUBLANES, stage1, 0)

    def stage2(k1, carry):
        g = g_ref[k1]
        a = jnp.concatenate([a_s[0, k1], a_s[1, k1]], axis=0).astype(BF16)
        x = _dot(g, a)
        xr, xi = x[:FFT_N2], x[FFT_N2:]
        rows = pl.ds(pl.multiple_of(k1 * FFT_N2, FFT_N2), FFT_N2)
        kr, ki = kh_ref[0, rows, :], kh_ref[1, rows, :]
        z = jnp.concatenate([xr * kr - xi * ki, xr * ki + xi * kr], axis=0).astype(BF16)
        d = lax.dot_general(g, z, (((0,), (0,)), ((), ())), preferred_element_type=F32)
        a_s[0, k1] = d[:FFT_N2]
        a_s[1, k1] = d[FFT_N2:]
        return carry

    lax.fori_loop(0, HY_SLABS, stage2, 0)

    bias = bias_ref[...]

    def stage3(j, carry):
        n2 = pl.ds(pl.multiple_of(j * SUBLANES, SUBLANES), SUBLANES)
        d = jnp.concatenate([a_s[0, :, n2, :], a_s[1, :, n2, :]], axis=0)
        d = d.reshape(2 * HY_SLABS_PAD * SUBLANES, LANES).astype(BF16)
        y = _dot(m3_ref[...], d).reshape(FFT_HALF, SUBLANES, LANES)
        o_ref[0, :, n2, :] = x0_s[:, n2, :] * (y + uu_s[:, n2, :] * bias)
        return carry

    lax.fori_loop(0, FFT_N2 // SUBLANES, stage3, 0)


def _hyena_mixer(main, b, conv_w, conv_b, khat, hy_bias):
    nchunk = D_HY // LANES
    m1, m3 = _hyena_stage_constants()
    _, g = _dft_constants()
    main4 = main.reshape(b, FFT_HALF, FFT_N2, D_MAIN)
    blk = lambda part: pl.BlockSpec((1, FFT_HALF, FFT_N2, LANES),
                                    lambda c, i, part=part: (i, 0, 0, part * nchunk + c))
    const = lambda a: pl.BlockSpec(a.shape, lambda c, i: (0,) * a.ndim, pipeline_mode=pl.Buffered(1))
    rows = HY_SLABS_PAD * FFT_N2
    ya = pl.pallas_call(
        _hyena_kernel,
        grid=(nchunk, b),
        in_specs=[blk(0), blk(1), blk(2),
                  pl.BlockSpec((3, 3, LANES), lambda c, i: (0, 0, c)),
                  pl.BlockSpec((3, 1, LANES), lambda c, i: (0, 0, c)),
                  const(m1), const(m3),
                  pl.BlockSpec((HY_SLABS_PAD, 2 * FFT_N2, 2 * FFT_N2), lambda c, i: (0, 0, 0),
                               pipeline_mode=pl.Buffered(1)),
                  pl.BlockSpec((2, rows, LANES), lambda c, i: (0, 0, c), pipeline_mode=pl.Buffered(1)),
                  pl.BlockSpec((1, LANES), lambda c, i: (0, c))],
        out_specs=pl.BlockSpec((1, FFT_HALF, FFT_N2, LANES), lambda c, i: (i, 0, 0, c)),
        out_shape=jax.ShapeDtypeStruct((b, FFT_HALF, FFT_N2, D_HY), F32),
        scratch_shapes=[pltpu.VMEM((FFT_HALF, FFT_N2, LANES), F32),
                        pltpu.VMEM((FFT_HALF, FFT_N2, LANES), F32),
                        pltpu.VMEM((2, HY_SLABS_PAD, FFT_N2, LANES), F32)],
        compiler_params=_params("parallel", "parallel"),
        name="hyena",
    )(main4, main4, main4, conv_w.reshape(3, 3, D_HY), conv_b.reshape(3, 1, D_HY), m1, m3, g, khat,
      hy_bias.reshape(1, D_HY))
    return ya.reshape(b * SEQ, D_HY)


ATT_TQ = 128
ATT_WIN = ATT_TQ + 2 * RADIUS
ATT_QC = 1024


def _attn_kernel(q_ref, kp_ref, k_ref, kn_ref, vp_ref, v_ref, vn_ref, o_ref, lse_ref, qs, ks, vs, *, n, qc):
    c0 = pl.program_id(2) * qc
    q = q_ref[0, 0]
    lane = lax.broadcasted_iota(jnp.int32, (qc, D_GROUP), 1)
    even_head = (lane % LANES) < HEAD_DIM
    qs[0] = jnp.where(even_head, q, 0.0).astype(BF16)
    qs[1] = jnp.where(even_head, 0.0, q).astype(BF16)
    ks[0:RADIUS] = kp_ref[0, 0].astype(BF16)
    ks[RADIUS:RADIUS + qc] = k_ref[0, 0].astype(BF16)
    ks[RADIUS + qc:] = kn_ref[0, 0].astype(BF16)
    vs[0:RADIUS] = vp_ref[0, 0].astype(BF16)
    vs[RADIUS:RADIUS + qc] = v_ref[0, 0].astype(BF16)
    vs[RADIUS + qc:] = vn_ref[0, 0].astype(BF16)

    lane_p = lax.broadcasted_iota(jnp.int32, (ATT_TQ, LANES), 1)
    low_head = lane_p < HEAD_DIM
    row = lax.broadcasted_iota(jnp.int32, (ATT_TQ, ATT_WIN), 0)
    col = lax.broadcasted_iota(jnp.int32, (ATT_TQ, ATT_WIN), 1)
    band = jnp.abs(col - RADIUS - row) - RADIUS

    def body(i, carry):
        q0 = pl.multiple_of(i * ATT_TQ, ATT_TQ)
        pos = col + (c0 + q0 - RADIUS)
        mask = jnp.maximum(jnp.maximum(band, -pos), pos - (n - 1)) <= 0
        for hp in range(D_GROUP // LANES):
            ls = slice(hp * LANES, (hp + 1) * LANES)
            kp = ks[pl.ds(q0, ATT_WIN), ls]
            vp = vs[pl.ds(q0, ATT_WIN), ls]
            outs, lses = [], []
            for h in range(2):
                s = lax.dot_general(qs[h, pl.ds(q0, ATT_TQ), ls], kp, (((1,), (1,)), ((), ())),
                                    preferred_element_type=F32)
                s = jnp.where(mask, s, NEG_INF)
                m = jnp.max(s, axis=-1, keepdims=True)
                e = jnp.where(mask, jnp.exp(s - m), 0.0)
                den = jnp.maximum(jnp.sum(e, axis=-1, keepdims=True), DEN_FLOOR)
                outs.append(_dot(e.astype(BF16), vp) / den)
                lses.append(m + jnp.log(den))
            o_ref[0, 0, pl.ds(q0, ATT_TQ), ls] = jnp.where(low_head, outs[0], outs[1])
            lse_ref[0, 0, pl.ds(q0, ATT_TQ), ls] = jnp.where(low_head, lses[0], lses[1])
        return carry

    lax.fori_loop(0, qc // ATT_TQ, body, 0)


def _dilated_attention_group(att, dil):
    b = att.shape[0]
    n = SEQ // dil
    qc = min(n, ATT_QC)
    hpc = qc // RADIUS
    last = n // RADIUS - 1

    def halo_specs(col):
        return [pl.BlockSpec((1, 1, RADIUS, D_GROUP), lambda i, r, c: (i, r, jnp.maximum(c * hpc - 1, 0), col)),
                pl.BlockSpec((1, 1, qc, D_GROUP), lambda i, r, c: (i, r, c, col)),
                pl.BlockSpec((1, 1, RADIUS, D_GROUP), lambda i, r, c: (i, r, jnp.minimum((c + 1) * hpc, last), col))]

    out = pl.BlockSpec((1, 1, qc, D_GROUP), lambda i, r, c: (i, r, c, 0))
    return pl.pallas_call(
        functools.partial(_attn_kernel, n=n, qc=qc),
        grid=(b, dil, n // qc),
        in_specs=[halo_specs(0)[1]] + halo_specs(1) + halo_specs(2),
        out_specs=[out, out],
        out_shape=[jax.ShapeDtypeStruct((b, dil, n, D_GROUP), F32)] * 2,
        scratch_shapes=[pltpu.VMEM((2, qc, D_GROUP), BF16), pltpu.VMEM((qc + 2 * RADIUS, D_GROUP), BF16),
                        pltpu.VMEM((qc + 2 * RADIUS, D_GROUP), BF16)],
        compiler_params=_params("parallel", "parallel", "parallel"),
        name=f"dilated_attention_d{dil}",
    )(att, att, att, att, att, att, att)


RG_ROWS = 512
RG_SEG = SEQ // SUBLANES
EXPM1_SERIES_BELOW = 0.125
EXPM1_SERIES_TERMS = 7


def _one_minus_exp(y, exp_y):
    ys = jnp.maximum(y, -EXPM1_SERIES_BELOW)
    poly = 1.0 + ys * (1.0 / EXPM1_SERIES_TERMS)
    for k in range(EXPM1_SERIES_TERMS - 1, 1, -1):
        poly = 1.0 + ys * (1.0 / k) * poly
    return jnp.where(y > -EXPM1_SERIES_BELOW, -ys * poly, 1.0 - exp_y)


def _rglru_kernel(xr_ref, gate_ref, cw_ref, cb_ref, wg_ref, bg_ref, lam_ref, o_ref, a_s, x_s, h_s, p_s):
    x_s[0] = _conv_rows(xr_ref[0], cw_ref[...], cb_ref[...], 2)
    lam = lam_ref[...]
    softplus = jnp.maximum(-lam, 0.0) + jnp.log1p(jnp.exp(-jnp.abs(lam)))
    wg = wg_ref[0]
    bg = bg_ref[0]

    def gates(i, carry):
        r0 = pl.multiple_of(i * RG_ROWS, RG_ROWS)
        xr = x_s[0, pl.ds(r0, RG_ROWS), :]
        g = _dot(xr.astype(BF16), wg) + bg
        row = lax.broadcasted_iota(jnp.int32, xr.shape, 0) + r0
        for d in range(2):
            r = jax.nn.sigmoid(g[:, (2 * d) * LANES:(2 * d + 1) * LANES])
            ig = jax.nn.sigmoid(g[:, (2 * d + 1) * LANES:(2 * d + 2) * LANES])
            log_a = -RG_C * r * softplus[d:d + 1, :]
            a = jnp.exp(log_a)
            mult = jnp.sqrt(_one_minus_exp(2.0 * log_a, a * a))
            start = 0 if d == 0 else SEQ - 1
            mult = jnp.where(row == start, 1.0, mult)
            a_s[d, pl.ds(r0, RG_ROWS), :] = a
            p_s[d, pl.ds(r0, RG_ROWS), :] = xr * ig * mult
        return carry

    lax.fori_loop(0, SEQ // RG_ROWS, gates, 0)

    def scan(j, carry):
        hf, pf, hb, pb = carry
        jb = RG_SEG - 1 - j
        idx_f = pl.ds(j, SUBLANES, stride=RG_SEG)
        idx_b = pl.ds(jb, SUBLANES, stride=RG_SEG)
        af = a_s[0, idx_f, :]
        hf = af * hf + p_s[0, idx_f, :]
        pf = pf * af
        h_s[0, idx_f, :] = hf
        x_s[0, idx_f, :] = pf
        ab = a_s[1, idx_b, :]
        hb = ab * hb + p_s[1, idx_b, :]
        pb = pb * ab
        h_s[1, idx_b, :] = hb
        x_s[1, idx_b, :] = pb
        return hf, pf, hb, pb

    zero = jnp.zeros((SUBLANES, LANES), F32)
    one = jnp.ones((SUBLANES, LANES), F32)
    hf, pf, hb, pb = lax.fori_loop(0, RG_SEG, scan, (zero, one, zero, one), unroll=8)

    sub = lax.broadcasted_iota(jnp.int32, (SUBLANES, LANES), 0)
    cf = zero
    for s in range(1, SUBLANES):
        cf = jnp.where(sub == s, pltpu.roll(hf + pf * cf, 1, axis=0), cf)
    cb = zero
    for s in range(SUBLANES - 2, -1, -1):
        cb = jnp.where(sub == s, pltpu.roll(hb + pb * cb, SUBLANES - 1, axis=0), cb)

    for s in range(SUBLANES):
        rows = pl.ds(s * RG_SEG, RG_SEG)
        h = (h_s[0, rows, :] + x_s[0, rows, :] * cf[s:s + 1, :]
             + h_s[1, rows, :] + x_s[1, rows, :] * cb[s:s + 1, :])
        o_ref[0, rows, :] = h * jax.nn.gelu(gate_ref[0, rows, :])


def _rglru_gate_weights(gate_w, gate_b):
    nchunk = D_RG // LANES
    bd = D_RG // RG_BLOCKS
    per = LANES // bd
    w = jnp.zeros((nchunk, LANES, 4, LANES), F32)
    for c in range(nchunk):
        for j in range(per):
            blk = gate_w[:, :, c * per + j].reshape(4, bd, bd)
            w = w.at[c, j * bd:(j + 1) * bd, :, j * bd:(j + 1) * bd].set(jnp.transpose(blk, (1, 0, 2)))
    w = w.reshape(nchunk, LANES, 4 * LANES).astype(BF16)
    b = gate_b.reshape(4, nchunk, LANES).transpose(1, 0, 2).reshape(nchunk, 1, 4 * LANES)
    return w, b


def _rglru_mixer(main, b, conv_w, conv_b, gate_w, gate_b, lam):
    nchunk = D_RG // LANES
    wg, bg = _rglru_gate_weights(gate_w, gate_b)
    main3 = main.reshape(b, SEQ, D_MAIN)
    blk = lambda off: pl.BlockSpec((1, SEQ, LANES), lambda i, c, off=off: (i, 0, off // LANES + c))
    yc = pl.pallas_call(
        _rglru_kernel,
        grid=(b, nchunk),
        in_specs=[blk(MAIN_RG), blk(MAIN_RG_GATE),
                  pl.BlockSpec((conv_w.shape[0], LANES), lambda i, c: (0, c)),
                  pl.BlockSpec((1, LANES), lambda i, c: (0, c)),
                  pl.BlockSpec((1, LANES, 4 * LANES), lambda i, c: (c, 0, 0)),
                  pl.BlockSpec((1, 1, 4 * LANES), lambda i, c: (c, 0, 0)),
                  pl.BlockSpec((2, LANES), lambda i, c: (0, c))],
        out_specs=pl.BlockSpec((1, SEQ, LANES), lambda i, c: (i, 0, c)),
        out_shape=jax.ShapeDtypeStruct((b, SEQ, D_RG), F32),
        scratch_shapes=[pltpu.VMEM((2, SEQ, LANES), F32)] * 4,
        compiler_params=_params("parallel", "parallel"),
        name="rglru",
    )(main3, main3, conv_w, conv_b.reshape(1, D_RG), wg, bg, lam)
    return yc.reshape(b * SEQ, D_RG)


MERGE_TM = 256


def _layer_norm(z, g, b):
    mu = jnp.mean(z, axis=-1, keepdims=True)
    zc = z - mu
    var = jnp.mean(zc * zc, axis=-1, keepdims=True)
    return zc * lax.rsqrt(var + LN_EPS) * g + b


def _time_major(ref, dil, scratch):
    if dil == 1:
        return ref[0, 0]
    tiles = D_GROUP // LANES
    for r in range(dil):
        for h in range(tiles):
            scratch[h, pl.ds(r, MERGE_TM // dil, stride=dil), :] = ref[0, r, :, h * LANES:(h + 1) * LANES]
    return jnp.concatenate([scratch[h] for h in range(tiles)], axis=1)


def _merge_kernel(x_ref, ya_ref, o0_ref, o1_ref, o2_ref, l0_ref, l1_ref, l2_ref, yc_ref,
                  wg_ref, bgate_ref, wa_ref, wb_ref, wc_ref, wo_ref, g_ref, b_ref, out_ref,
                  s0, s1, s2, s3):
    x = x_ref[...]
    xb = x.astype(BF16)
    d0, d1, d2 = DILATIONS
    o0, l0 = _time_major(o0_ref, d0, s0), _time_major(l0_ref, d0, s0)
    o1, l1 = _time_major(o1_ref, d1, s0), _time_major(l1_ref, d1, s1)
    o2, l2 = _time_major(o2_ref, d2, s2), _time_major(l2_ref, d2, s3)
    lmax = jnp.maximum(jnp.maximum(l0, l1), l2)
    e0, e1, e2 = jnp.exp(l0 - lmax), jnp.exp(l1 - lmax), jnp.exp(l2 - lmax)
    yb = (e0 * o0 + e1 * o1 + e2 * o2) / (e0 + e1 + e2)
    branches = ((ya_ref[...], wa_ref), (yb, wb_ref), (yc_ref[...], wc_ref))
    mixed = None
    for j, (y, w_ref) in enumerate(branches):
        cols = slice(j * D_MODEL, (j + 1) * D_MODEL)
        gate = jax.nn.sigmoid(_dot(xb, wg_ref[:, cols]) + bgate_ref[:, cols])
        term = gate * _dot(y.astype(BF16), w_ref[...])
        mixed = term if mixed is None else mixed + term
    z = DN_ALPHA * x + _dot(mixed.astype(BF16), wo_ref[...])
    out_ref[...] = _layer_norm(z, g_ref[...], b_ref[...])


def _merge_and_project(x2d, ya, att, yc, w_gate, b_gate, w_a, w_b, w_c, w_o, ln_g, ln_b):
    t = x2d.shape[0]
    tps = SEQ // MERGE_TM
    rows = lambda w: pl.BlockSpec((MERGE_TM, w), lambda i: (i, 0))
    grp = lambda d: pl.BlockSpec((1, d, MERGE_TM // d, D_GROUP), lambda i: (i // tps, 0, i % tps, 0))
    const = lambda a: pl.BlockSpec(a.shape, lambda i: (0,) * a.ndim, pipeline_mode=pl.Buffered(1))
    (o0, l0), (o1, l1), (o2, l2) = att
    weights = (w_gate, b_gate.reshape(1, -1), w_a, w_b, w_c, w_o, ln_g.reshape(1, -1), ln_b.reshape(1, -1))
    return pl.pallas_call(
        _merge_kernel,
        grid=(t // MERGE_TM,),
        in_specs=[rows(D_MODEL), rows(D_HY)] + [grp(d) for d in DILATIONS] * 2 + [rows(D_RG)]
                 + [const(a) for a in weights],
        out_specs=rows(D_MODEL),
        out_shape=jax.ShapeDtypeStruct((t, D_MODEL), F32),
        scratch_shapes=[pltpu.VMEM((D_GROUP // LANES, MERGE_TM, LANES), F32)] * 4,
        compiler_params=_params("parallel"),
        name="merge_project_norm",
    )(x2d, ya, o0, o1, o2, l0, l1, l2, yc, *weights)


FFN_TM = 512
FFN_CF = 512


def _ffn_kernel(x_ref, prev_ref, next_ref, wup_ref, cw_ref, cb_ref, wdn_ref, g_ref, b_ref, out_ref):
    i = pl.program_id(0)
    tiles_per_seq = SEQ // FFN_TM
    x = x_ref[...]
    xb = x.astype(BF16)
    has_prev = (i % tiles_per_seq) != 0
    has_next = (i % tiles_per_seq) != tiles_per_seq - 1
    halo = jnp.concatenate([prev_ref[...], next_ref[...]], axis=0).astype(BF16)
    row = lax.broadcasted_iota(jnp.int32, (FFN_TM, FFN_CF), 0)
    acc = None
    for c in range(D_FF // FFN_CF):
        cols = slice(c * FFN_CF, (c + 1) * FFN_CF)
        ucols = slice(D_FF + c * FFN_CF, D_FF + (c + 1) * FFN_CF)
        hg = _dot(xb, wup_ref[:, cols])
        hh = _dot(halo, wup_ref[:, cols])
        before = jnp.where(has_prev, hh[SUBLANES - 1:SUBLANES, :], 0.0)
        after = jnp.where(has_next, hh[SUBLANES:SUBLANES + 1, :], 0.0)
        up = jnp.where(row == 0, before, pltpu.roll(hg, 1, axis=0))
        dn = jnp.where(row == FFN_TM - 1, after, pltpu.roll(hg, FFN_TM - 1, axis=0))
        w = cw_ref[:, cols]
        conv = up * w[0:1, :] + hg * w[1:2, :] + dn * w[2:3, :] + cb_ref[:, cols]
        act = jax.nn.gelu(conv) * _dot(xb, wup_ref[:, ucols])
        term = _dot(act.astype(BF16), wdn_ref[cols, :])
        acc = term if acc is None else acc + term
    out_ref[...] = _layer_norm(DN_ALPHA * x + acc, g_ref[...], b_ref[...])


def _ffn(x2d, w_up, conv_w, conv_b, w_down, ln_g, ln_b):
    t = x2d.shape[0]
    bpt = FFN_TM // SUBLANES
    nblk = t // SUBLANES
    const = lambda a: pl.BlockSpec(a.shape, lambda i: (0,) * a.ndim, pipeline_mode=pl.Buffered(1))
    weights = (w_up, conv_w, conv_b.reshape(1, -1), w_down, ln_g.reshape(1, -1), ln_b.reshape(1, -1))
    return pl.pallas_call(
        _ffn_kernel,
        grid=(t // FFN_TM,),
        in_specs=[pl.BlockSpec((FFN_TM, D_MODEL), lambda i: (i, 0)),
                  pl.BlockSpec((SUBLANES, D_MODEL), lambda i: (jnp.maximum(i * bpt - 1, 0), 0)),
                  pl.BlockSpec((SUBLANES, D_MODEL), lambda i: (jnp.minimum((i + 1) * bpt, nblk - 1), 0))]
                 + [const(a) for a in weights],
        out_specs=pl.BlockSpec((FFN_TM, D_MODEL), lambda i: (i, 0)),
        out_shape=jax.ShapeDtypeStruct((t, D_MODEL), F32),
        compiler_params=_params("parallel"),
        name="ffn_norm",
    )(x2d, x2d, x2d, *weights)


def _encoder_layer(x2d, b, l, p, khat):
    main, *att = _input_projection(x2d, p["w_in"][l])
    ya = _hyena_mixer(main, b, p["hy_conv_w"][l], p["hy_conv_b"][l], khat[l], p["hy_bias"][l])
    att_out = [_dilated_attention_group(a, dil) for a, dil in zip(att, DILATIONS)]
    yc = _rglru_mixer(main, b, p["rg_conv_w"][l], p["rg_conv_b"][l], p["rg_gate_w"][l], p["rg_gate_b"][l],
                      p["rg_lam"][l])
    x2d = _merge_and_project(x2d, ya, att_out, yc,
                             p["w_gate"][l], p["b_gate"][l], p["w_br_a"][l], p["w_br_b"][l], p["w_br_c"][l],
                             p["w_o"][l], p["ln1_g"][l], p["ln1_b"][l])
    return _ffn(x2d, p["w_up"][l], p["ffn_conv_w"][l], p["ffn_conv_b"][l], p["w_down"][l],
                p["ln2_g"][l], p["ln2_b"][l])


def _trunk(x, p):
    b = x.shape[0]
    khat = _hyena_filter_spectra(p["hy_filt_w1"], p["hy_filt_b1"], p["hy_filt_w2"], p["hy_filt_b2"],
                                 p["hy_filt_w3"], p["hy_filt_b3"], p["hy_filt_freq"])
    x2d = x.reshape(b * SEQ, D_MODEL)
    for l in range(DEPTH):
        x2d = _encoder_layer(x2d, b, l, p, khat)
    return x2d.reshape(b, SEQ, D_MODEL)


_MATMUL_WEIGHTS = ("w_in", "w_gate", "w_br_a", "w_br_b", "w_br_c", "w_o", "w_up", "w_down")


def kernel(x_prompt, x_sample, w_in, hy_conv_w, hy_conv_b, hy_filt_w1, hy_filt_b1, hy_filt_w2, hy_filt_b2, hy_filt_w3, hy_filt_b3, hy_filt_freq, hy_bias, rg_conv_w, rg_conv_b, rg_gate_w, rg_gate_b, rg_lam, w_gate, b_gate, w_br_a, w_br_b, w_br_c, w_o, ln1_g, ln1_b, w_up, ffn_conv_w, ffn_conv_b, w_down, ln2_g, ln2_b):
    p = dict(w_in=w_in, hy_conv_w=hy_conv_w, hy_conv_b=hy_conv_b, hy_filt_w1=hy_filt_w1,
             hy_filt_b1=hy_filt_b1, hy_filt_w2=hy_filt_w2, hy_filt_b2=hy_filt_b2, hy_filt_w3=hy_filt_w3,
             hy_filt_b3=hy_filt_b3, hy_filt_freq=hy_filt_freq, hy_bias=hy_bias, rg_conv_w=rg_conv_w,
             rg_conv_b=rg_conv_b, rg_gate_w=rg_gate_w, rg_gate_b=rg_gate_b, rg_lam=rg_lam, w_gate=w_gate,
             b_gate=b_gate, w_br_a=w_br_a, w_br_b=w_br_b, w_br_c=w_br_c, w_o=w_o, ln1_g=ln1_g, ln1_b=ln1_b,
             w_up=w_up, ffn_conv_w=ffn_conv_w, ffn_conv_b=ffn_conv_b, w_down=w_down, ln2_g=ln2_g,
             ln2_b=ln2_b)
    for name in _MATMUL_WEIGHTS:
        p[name] = p[name].astype(BF16)
    nb = x_prompt.shape[0]
    y = _trunk(jnp.concatenate([x_prompt, x_sample], axis=0), p)
    return (y[:nb], y[nb:])
```

```python
import functools
import math

import numpy as np
import jax
import jax.numpy as jnp
from jax import lax
from jax.experimental import pallas as pl
from jax.experimental.pallas import tpu as pltpu

F32 = jnp.float32
BF16 = jnp.bfloat16

D_MODEL = 1024
SEQ = 4096
DEPTH = 2
D_HY = 512
HY_EMB_BANDS = 8
HY_EMB_PAD = 32
HY_FAST_DECAY = 0.3
HY_SLOW_DECAY = 1.5
HY_TARGET = 1e-2
HEAD_DIM = 64
HEADS_PER_GROUP = 4
DILATIONS = (1, 4, 16)
RADIUS = 64
D_GROUP = HEADS_PER_GROUP * HEAD_DIM
D_ATT = len(DILATIONS) * D_GROUP
ROPE_THETA = 10000.0
NEG_INF = -1e30
DEN_FLOOR = 1e-30
D_RG = 512
RG_BLOCKS = 8
RG_C = 8.0
D_FF = 3 * D_MODEL
DN_ALPHA = (2 * DEPTH) ** 0.25
LN_EPS = 1e-5
D_IN = 3 * D_HY + 3 * D_ATT + 2 * D_RG
OFF_Q = 3 * D_HY
OFF_K = OFF_Q + D_ATT
OFF_V = OFF_K + D_ATT
OFF_RG = OFF_V + D_ATT
D_MAIN = 3 * D_HY + 2 * D_RG
MAIN_RG = 3 * D_HY
MAIN_RG_GATE = MAIN_RG + D_RG

LANES = 128
SUBLANES = 8
VMEM_LIMIT = 56 * 1024 * 1024

FFT_N = 2 * SEQ
FFT_N1 = 64
FFT_N2 = 128
FFT_HALF = FFT_N1 // 2
HY_COLS = FFT_N2 * D_HY
HY_SLABS = FFT_N1 // 2 + 1
HY_SLABS_PAD = 40


def _params(*sem):
    return pltpu.CompilerParams(dimension_semantics=sem, vmem_limit_bytes=VMEM_LIMIT)


def _dot(a, b):
    return jnp.dot(a, b, preferred_element_type=F32)


def _cos_sin(a, b, period):
    ang = 2.0 * np.pi * (np.outer(a, b) % period) / period
    return np.cos(ang), np.sin(ang)


@functools.lru_cache(maxsize=None)
def _dft_constants():
    k1 = np.arange(FFT_N1)
    c, s = _cos_sin(k1, np.arange(FFT_N1), FFT_N1)
    f1_full = np.concatenate([c, -s], axis=0)
    n2 = np.arange(FFT_N2)
    k2 = np.arange(FFT_N2)
    g = np.zeros((FFT_N1, 2 * FFT_N2, 2 * FFT_N2), np.float64)
    for a in range(FFT_N1):
        c, s = _cos_sin(a + FFT_N1 * k2, n2, FFT_N)
        gr, gi = c, -s
        g[a] = np.block([[gr, -gi], [gi, gr]])
    return jnp.asarray(f1_full, BF16), jnp.asarray(g, BF16)


@functools.lru_cache(maxsize=None)
def _hyena_stage_constants():
    k1 = np.arange(HY_SLABS_PAD)
    n1 = np.arange(FFT_HALF)
    eye = np.eye(SUBLANES)
    c, s = _cos_sin(k1, n1, FFT_N1)
    m1 = np.kron(np.concatenate([c, -s], axis=0), eye)
    wgt = np.where((k1 == 0) | (k1 == FFT_N1 // 2), 1.0, 2.0) * (k1 < HY_SLABS)
    c, s = _cos_sin(n1, k1, FFT_N1)
    m3 = np.kron(np.concatenate([c * wgt, -s * wgt], axis=1) / FFT_N, eye)
    return jnp.asarray(m1, BF16), jnp.asarray(m3, BF16)


@functools.lru_cache(maxsize=None)
def _filter_positions():
    L = SEQ
    t = np.linspace(0.0, 1.0, L, dtype=np.float32).astype(np.float64)[:, None]
    w = (2.0 * np.pi * np.arange(L, dtype=np.float32) / L).astype(np.float64)[:, None]
    bands = np.linspace(1e-4, HY_EMB_BANDS - 1, HY_EMB_BANDS, dtype=np.float32).astype(np.float64)[None, :]
    z = np.concatenate([t, np.cos(bands * w), -np.sin(bands * w)], axis=-1)
    m = np.arange(2 * L)
    src = np.where(m < L, m, 2 * L - m)
    src = np.where(m == L, 0, src)
    zp = np.zeros((2 * L, HY_EMB_PAD))
    zp[:, :z.shape[1]] = z[src]
    return jnp.asarray(zp, F32)


@functools.lru_cache(maxsize=None)
def _rope_tables():
    inv = ROPE_THETA ** (-np.arange(0, HEAD_DIM, 2, dtype=np.float32).astype(np.float64) / HEAD_DIM)
    ang = np.arange(SEQ, dtype=np.float64)[:, None] * inv[None, :]
    cos = np.concatenate([np.cos(ang), np.cos(ang)], axis=1)
    sin = np.concatenate([-np.sin(ang), np.sin(ang)], axis=1)
    cos = np.tile(cos, (1, HEADS_PER_GROUP))
    sin = np.tile(sin, (1, HEADS_PER_GROUP))
    return jnp.asarray(cos, F32), jnp.asarray(sin, F32)


PROJ_TM = 512
_MAIN_CHUNKS = tuple((src, dst, 512) for src, dst in
                     [(j, j) for j in range(0, 3 * D_HY, 512)] +
                     [(OFF_RG + j, MAIN_RG + j) for j in range(0, 2 * D_RG, 512)])


def _rope(t, cos, sin):
    lane = lax.broadcasted_iota(jnp.int32, t.shape, 1)
    first_half = (lane % HEAD_DIM) < (HEAD_DIM // 2)
    rot = jnp.where(first_half, pltpu.roll(t, D_GROUP - HEAD_DIM // 2, axis=1),
                    pltpu.roll(t, HEAD_DIM // 2, axis=1))
    return t * cos + rot * sin


def _proj_kernel(x_ref, w_ref, cos_ref, sin_ref, main_ref, a0_ref, a1_ref, a2_ref, qkv_s):
    xb = x_ref[...].astype(BF16)
    for src, dst, width in _MAIN_CHUNKS:
        main_ref[:, dst:dst + width] = _dot(xb, w_ref[:, src:src + width])
    cos = cos_ref[...]
    sin = sin_ref[...]
    for g, (dil, att_ref) in enumerate(zip(DILATIONS, (a0_ref, a1_ref, a2_ref))):
        col = lambda off: slice(off + g * D_GROUP, off + (g + 1) * D_GROUP)
        q = _rope(_dot(xb, w_ref[:, col(OFF_Q)]), cos, sin) * (HEAD_DIM ** -0.5)
        k = _rope(_dot(xb, w_ref[:, col(OFF_K)]), cos, sin)
        v = _dot(xb, w_ref[:, col(OFF_V)])
        if dil == 1:
            att_ref[0, 0, :, 0:D_GROUP] = q
            att_ref[0, 0, :, D_GROUP:2 * D_GROUP] = k
            att_ref[0, 0, :, 2 * D_GROUP:] = v
        else:
            for j, t in enumerate((q, k, v)):
                for h in range(D_GROUP // LANES):
                    qkv_s[j * (D_GROUP // LANES) + h] = t[:, h * LANES:(h + 1) * LANES]
            for r in range(dil):
                for j in range(3 * D_GROUP // LANES):
                    att_ref[0, r, :, j * LANES:(j + 1) * LANES] = qkv_s[j, pl.ds(r, PROJ_TM // dil, stride=dil), :]


def _input_projection(x2d, w_in_bf16):
    t = x2d.shape[0]
    b = t // SEQ
    tps = SEQ // PROJ_TM
    cos, sin = _rope_tables()
    tab = pl.BlockSpec((PROJ_TM, D_GROUP), lambda i: (i % tps, 0))
    att_spec = lambda d: pl.BlockSpec((1, d, PROJ_TM // d, 3 * D_GROUP), lambda i: (i // tps, 0, i % tps, 0))
    att_shape = lambda d: jax.ShapeDtypeStruct((b, d, SEQ // d, 3 * D_GROUP), F32)
    return pl.pallas_call(
        _proj_kernel,
        grid=(t // PROJ_TM,),
        in_specs=[pl.BlockSpec((PROJ_TM, D_MODEL), lambda i: (i, 0)),
                  pl.BlockSpec((D_MODEL, D_IN), lambda i: (0, 0), pipeline_mode=pl.Buffered(1)),
                  tab, tab],
        out_specs=[pl.BlockSpec((PROJ_TM, D_MAIN), lambda i: (i, 0))] + [att_spec(d) for d in DILATIONS],
        out_shape=[jax.ShapeDtypeStruct((t, D_MAIN), F32)] + [att_shape(d) for d in DILATIONS],
        scratch_shapes=[pltpu.VMEM((3 * D_GROUP // LANES, PROJ_TM, LANES), F32)],
        compiler_params=_params("parallel"),
        name="input_projection",
    )(x2d, w_in_bf16, cos, sin)


def _conv_rows(x, w, bias, pad_left):
    n = x.shape[0]
    row = lax.broadcasted_iota(jnp.int32, x.shape, 0)
    acc = None
    for k in range(w.shape[0]):
        off = k - pad_left
        if off == 0:
            xs = x
        else:
            xs = pltpu.roll(x, (-off) % n, axis=0)
            valid = (row >= -off) if off < 0 else (row < n - off)
            xs = jnp.where(valid, xs, 0.0)
        term = xs * w[k:k + 1, :]
        acc = term if acc is None else acc + term
    return acc + bias


FILT_ROWS = 512


def _hy_filter_kernel(z_ref, w1_ref, b1_ref, w2_ref, b2_ref, w3f_ref, w3b_ref, b3f_ref, b3b_ref,
                      fr_ref, dl_ref, k_ref, h_s):
    hi = lax.Precision.HIGHEST
    w3f, w3b, b3f, b3b = w3f_ref[0], w3b_ref[0], b3f_ref[0], b3b_ref[0]
    delta = dl_ref[...]

    @pl.when(pl.program_id(1) == 0)
    def _():
        fr = fr_ref[0]
        w1, b1, w2, b2 = w1_ref[0], b1_ref[0], w2_ref[0], b2_ref[0]

        def hidden(i, carry):
            rows = pl.ds(pl.multiple_of(i * FILT_ROWS, FILT_ROWS), FILT_ROWS)
            h = jnp.sin(fr * (jnp.dot(z_ref[rows, :], w1, precision=hi, preferred_element_type=F32) + b1))
            h_s[rows, :] = jnp.sin(fr * (jnp.dot(h, w2, precision=hi, preferred_element_type=F32) + b2))
            return carry

        lax.fori_loop(0, FFT_N // FILT_ROWS, hidden, 0)

    def body(i, total):
        r0 = pl.multiple_of(i * FILT_ROWS, FILT_ROWS)
        z = z_ref[pl.ds(r0, FILT_ROWS), :]
        h = h_s[pl.ds(r0, FILT_ROWS), :]
        hf = jnp.dot(h, w3f, precision=hi, preferred_element_type=F32) + b3f
        hb = jnp.dot(h, w3b, precision=hi, preferred_element_type=F32) + b3b
        m = lax.broadcasted_iota(jnp.int32, hf.shape, 0) + r0
        decay = jnp.exp(-z[:, 0:1] * delta)
        val = jnp.where(m < SEQ, hf, hb) * decay
        val = jnp.where(m == SEQ, 0.0, val)
        k_ref[0, pl.ds(r0, FILT_ROWS), :] = val
        return total + jnp.sum(jnp.abs(val), axis=0, keepdims=True)

    total = lax.fori_loop(0, FFT_N // FILT_ROWS, body, jnp.zeros((1, LANES), F32))

    def scale(i, carry):
        r0 = pl.multiple_of(i * FILT_ROWS, FILT_ROWS)
        k_ref[0, pl.ds(r0, FILT_ROWS), :] = k_ref[0, pl.ds(r0, FILT_ROWS), :] / total
        return carry

    lax.fori_loop(0, FFT_N // FILT_ROWS, scale, 0)


def _hyena_filters(w1, b1, w2, b2, w3, b3, freq):
    nchunk = D_HY // LANES
    z = _filter_positions()
    emb = z.shape[1]
    w1 = jnp.pad(w1, ((0, 0), (0, emb - w1.shape[1]), (0, 0)))
    hid = w1.shape[-1]
    deltas = np.abs(np.linspace(math.log(HY_TARGET) / HY_SLOW_DECAY, math.log(HY_TARGET) / HY_FAST_DECAY,
                                D_HY, dtype=np.float32))
    deltas = jnp.asarray(deltas, F32).reshape(1, D_HY)
    row = lambda a: a.reshape(DEPTH, 1, a.shape[-1])
    full = lambda s: pl.BlockSpec((1,) + s, lambda l, c: (l, 0, 0))
    return pl.pallas_call(
        _hy_filter_kernel,
        grid=(DEPTH, nchunk),
        in_specs=[pl.BlockSpec((FFT_N, emb), lambda l, c: (0, 0)),
                  full((emb, hid)), full((1, hid)), full((hid, hid)), full((1, hid)),
                  pl.BlockSpec((1, hid, LANES), lambda l, c: (l, 0, c)),
                  pl.BlockSpec((1, hid, LANES), lambda l, c: (l, 0, nchunk + c)),
                  pl.BlockSpec((1, 1, LANES), lambda l, c: (l, 0, c)),
                  pl.BlockSpec((1, 1, LANES), lambda l, c: (l, 0, nchunk + c)),
                  full((1, hid)),
                  pl.BlockSpec((1, LANES), lambda l, c: (0, c))],
        out_specs=pl.BlockSpec((1, FFT_N, LANES), lambda l, c: (l, 0, c)),
        out_shape=jax.ShapeDtypeStruct((DEPTH, FFT_N, D_HY), F32),
        scratch_shapes=[pltpu.VMEM((FFT_N, hid), F32)],
        compiler_params=_params("parallel", "arbitrary"),
        name="hyena_filter",
    )(z, w1, row(b1), w2, row(b2), w3, w3, row(b3), row(b3), row(freq), deltas)


DFT_TN = 8192


def _dft_rows_kernel(f_ref, x_ref, o_ref):
    o_ref[0] = _dot(f_ref[...], x_ref[0].astype(BF16))


def _dft_over_n1(x_view, f):
    b, k, cols = x_view.shape
    m = f.shape[0]
    return pl.pallas_call(
        _dft_rows_kernel,
        grid=(b, cols // DFT_TN),
        in_specs=[pl.BlockSpec((m, k), lambda i, j: (0, 0)),
                  pl.BlockSpec((1, k, DFT_TN), lambda i, j: (i, 0, j))],
        out_specs=pl.BlockSpec((1, m, DFT_TN), lambda i, j: (i, 0, j)),
        out_shape=jax.ShapeDtypeStruct((b, m, cols), F32),
        compiler_params=_params("parallel", "parallel"),
        name="hyena_dft_n1",
    )(f, x_view)


def _filter_spectrum_kernel(g_ref, a_ref, o_ref):
    a = a_ref[0].reshape(2 * FFT_N2, D_HY).astype(BF16)
    o_ref[0] = _dot(g_ref[0], a).reshape(2, FFT_N2, D_HY)


def _filter_spectrum(a_k, g):
    slab = pl.BlockSpec((1, 2, FFT_N2, D_HY), lambda k1, l: (l, 0, k1, 0))
    return pl.pallas_call(
        _filter_spectrum_kernel,
        grid=(FFT_N1, DEPTH),
        in_specs=[pl.BlockSpec((1, 2 * FFT_N2, 2 * FFT_N2), lambda k1, l: (k1, 0, 0)), slab],
        out_specs=slab,
        out_shape=jax.ShapeDtypeStruct(a_k.shape, F32),
        compiler_params=_params("parallel", "parallel"),
        name="hyena_filter_spectrum",
    )(g, a_k)


def _hyena_filter_spectra(w1, b1, w2, b2, w3, b3, freq):
    f1_full, g = _dft_constants()
    k = _hyena_filters(w1, b1, w2, b2, w3, b3, freq)
    a_k = _dft_over_n1(k.reshape(DEPTH, FFT_N1, HY_COLS), f1_full)
    return _filter_spectrum(a_k.reshape(DEPTH, 2, FFT_N, D_HY), g)


HY_SLAB_UNROLL = 3
HY_EDGE_UNROLL = 2


def _hyena_kernel(x0_ref, x1_ref, v_ref, cw_ref, cb_ref, m1_ref, m3_ref, g_ref, kh_ref, bias_ref, o_ref,
                  uu_s, x0_s, a_s):
    shape3 = (FFT_HALF, FFT_N2, LANES)
    nat = lambda ref: ref[0].reshape(SEQ, LANES)
    x0_s[...] = _conv_rows(nat(x0_ref), cw_ref[:, 0, :], cb_ref[0], 1).reshape(shape3)
    x1 = _conv_rows(nat(x1_ref), cw_ref[:, 1, :], cb_ref[1], 1)
    v = _conv_rows(nat(v_ref), cw_ref[:, 2, :], cb_ref[2], 1)
    uu_s[...] = (v * x1).reshape(shape3)

    def stage1(j, carry):
        blocks = [pl.ds(pl.multiple_of((j * HY_EDGE_UNROLL + u) * SUBLANES, SUBLANES), SUBLANES)
                  for u in range(HY_EDGE_UNROLL)]
        xs = [uu_s[:, n2, :].reshape(FFT_HALF * SUBLANES, LANES).astype(BF16) for n2 in blocks]
        res = [_dot(m1_ref[...], xb).reshape(2 * HY_SLABS_PAD, SUBLANES, LANES) for xb in xs]
        for n2, a in zip(blocks, res):
            a_s[0, :, n2, :] = a[:HY_SLABS_PAD]
            a_s[1, :, n2, :] = a[HY_SLABS_PAD:]
        return carry

    lax.fori_loop(0, FFT_N2 // SUBLANES // HY_EDGE_UNROLL, stage1, 0)

    def stage2(j, carry):
        slabs = [j * HY_SLAB_UNROLL + u for u in range(HY_SLAB_UNROLL)]
        gs = [g_ref[k1] for k1 in slabs]
        xs = [jnp.concatenate([a_s[0, k1], a_s[1, k1]], axis=0).astype(BF16) for k1 in slabs]
        res = []
        for k1, g, a in zip(slabs, gs, xs):
            x = _dot(g, a)
            xr, xi = x[:FFT_N2], x[FFT_N2:]
            rows = pl.ds(pl.multiple_of(k1 * FFT_N2, FFT_N2), FFT_N2)
            kr, ki = kh_ref[0, rows, :], kh_ref[1, rows, :]
            z = jnp.concatenate([xr * kr - xi * ki, xr * ki + xi * kr], axis=0).astype(BF16)
            res.append(lax.dot_general(g, z, (((0,), (0,)), ((), ())), preferred_element_type=F32))
        for k1, d in zip(slabs, res):
            a_s[0, k1] = d[:FFT_N2]
            a_s[1, k1] = d[FFT_N2:]
        return carry

    lax.fori_loop(0, HY_SLABS // HY_SLAB_UNROLL, stage2, 0)

    bias = bias_ref[...]

    def stage3(j, carry):
        blocks = [pl.ds(pl.multiple_of((j * HY_EDGE_UNROLL + u) * SUBLANES, SUBLANES), SUBLANES)
                  for u in range(HY_EDGE_UNROLL)]
        ds = [jnp.concatenate([a_s[0, :, n2, :], a_s[1, :, n2, :]], axis=0)
              .reshape(2 * HY_SLABS_PAD * SUBLANES, LANES).astype(BF16) for n2 in blocks]
        ys = [_dot(m3_ref[...], d).reshape(FFT_HALF, SUBLANES, LANES) for d in ds]
        for n2, y in zip(blocks, ys):
            o_ref[0, :, n2, :] = x0_s[:, n2, :] * (y + uu_s[:, n2, :] * bias)
        return carry

    lax.fori_loop(0, FFT_N2 // SUBLANES // HY_EDGE_UNROLL, stage3, 0)


def _hyena_mixer(main, b, conv_w, conv_b, khat, hy_bias):
    nchunk = D_HY // LANES
    m1, m3 = _hyena_stage_constants()
    _, g = _dft_constants()
    main4 = main.reshape(b, FFT_HALF, FFT_N2, D_MAIN)
    blk = lambda part: pl.BlockSpec((1, FFT_HALF, FFT_N2, LANES),
                                    lambda c, i, part=part: (i, 0, 0, part * nchunk + c))
    const = lambda a: pl.BlockSpec(a.shape, lambda c, i: (0,) * a.ndim, pipeline_mode=pl.Buffered(1))
    rows = HY_SLABS_PAD * FFT_N2
    ya = pl.pallas_call(
        _hyena_kernel,
        grid=(nchunk, b),
        in_specs=[blk(0), blk(1), blk(2),
                  pl.BlockSpec((3, 3, LANES), lambda c, i: (0, 0, c)),
                  pl.BlockSpec((3, 1, LANES), lambda c, i: (0, 0, c)),
                  const(m1), const(m3),
                  pl.BlockSpec((HY_SLABS_PAD, 2 * FFT_N2, 2 * FFT_N2), lambda c, i: (0, 0, 0),
                               pipeline_mode=pl.Buffered(1)),
                  pl.BlockSpec((2, rows, LANES), lambda c, i: (0, 0, c), pipeline_mode=pl.Buffered(1)),
                  pl.BlockSpec((1, LANES), lambda c, i: (0, c))],
        out_specs=pl.BlockSpec((1, FFT_HALF, FFT_N2, LANES), lambda c, i: (i, 0, 0, c)),
        out_shape=jax.ShapeDtypeStruct((b, FFT_HALF, FFT_N2, D_HY), F32),
        scratch_shapes=[pltpu.VMEM((FFT_HALF, FFT_N2, LANES), F32),
                        pltpu.VMEM((FFT_HALF, FFT_N2, LANES), F32),
                        pltpu.VMEM((2, HY_SLABS_PAD, FFT_N2, LANES), F32)],
        compiler_params=_params("parallel", "parallel"),
        name="hyena",
    )(main4, main4, main4, conv_w.reshape(3, 3, D_HY), conv_b.reshape(3, 1, D_HY), m1, m3, g, khat,
      hy_bias.reshape(1, D_HY))
    return ya.reshape(b * SEQ, D_HY)


ATT_TQ = 128
ATT_WIN = ATT_TQ + 2 * RADIUS
ATT_QC = 1024
ATT_UNROLL = 2


def _attn_kernel(q_ref, kp_ref, k_ref, kn_ref, vp_ref, v_ref, vn_ref, o_ref, lse_ref, qs, ks, vs, *, n, qc):
    c0 = pl.program_id(2) * qc
    q = q_ref[0, 0]
    lane = lax.broadcasted_iota(jnp.int32, (qc, D_GROUP), 1)
    even_head = (lane % LANES) < HEAD_DIM
    qs[0] = jnp.where(even_head, q, 0.0).astype(BF16)
    qs[1] = jnp.where(even_head, 0.0, q).astype(BF16)
    ks[0:RADIUS] = kp_ref[0, 0].astype(BF16)
    ks[RADIUS:RADIUS + qc] = k_ref[0, 0].astype(BF16)
    ks[RADIUS + qc:] = kn_ref[0, 0].astype(BF16)
    vs[0:RADIUS] = vp_ref[0, 0].astype(BF16)
    vs[RADIUS:RADIUS + qc] = v_ref[0, 0].astype(BF16)
    vs[RADIUS + qc:] = vn_ref[0, 0].astype(BF16)

    lane_p = lax.broadcasted_iota(jnp.int32, (ATT_TQ, LANES), 1)
    low_head = lane_p < HEAD_DIM
    row = lax.broadcasted_iota(jnp.int32, (ATT_TQ, ATT_WIN), 0)
    col = lax.broadcasted_iota(jnp.int32, (ATT_TQ, ATT_WIN), 1)
    band = jnp.abs(col - RADIUS - row) - RADIUS

    def query_block(q0):
        pos = col + (c0 + q0 - RADIUS)
        mask = jnp.maximum(jnp.maximum(band, -pos), pos - (n - 1)) <= 0
        for hp in range(D_GROUP // LANES):
            ls = slice(hp * LANES, (hp + 1) * LANES)
            kp = ks[pl.ds(q0, ATT_WIN), ls]
            vp = vs[pl.ds(q0, ATT_WIN), ls]
            outs, lses = [], []
            for h in range(2):
                s = lax.dot_general(qs[h, pl.ds(q0, ATT_TQ), ls], kp, (((1,), (1,)), ((), ())),
                                    preferred_element_type=F32)
                s = jnp.where(mask, s, NEG_INF)
                m = jnp.max(s, axis=-1, keepdims=True)
                e = jnp.where(mask, jnp.exp(s - m), 0.0)
                den = jnp.maximum(jnp.sum(e, axis=-1, keepdims=True), DEN_FLOOR)
                outs.append(_dot(e.astype(BF16), vp) / den)
                lses.append(m + jnp.log(den))
            o_ref[0, 0, pl.ds(q0, ATT_TQ), ls] = jnp.where(low_head, outs[0], outs[1])
            lse_ref[0, 0, pl.ds(q0, ATT_TQ), ls] = jnp.where(low_head, lses[0], lses[1])

    def body(i, carry):
        for u in range(ATT_UNROLL):
            query_block(pl.multiple_of((i * ATT_UNROLL + u) * ATT_TQ, ATT_TQ))
        return carry

    lax.fori_loop(0, qc // ATT_TQ // ATT_UNROLL, body, 0)


def _dilated_attention_group(att, dil):
    b = att.shape[0]
    n = SEQ // dil
    qc = min(n, ATT_QC)
    hpc = qc // RADIUS
    last = n // RADIUS - 1

    def halo_specs(col):
        return [pl.BlockSpec((1, 1, RADIUS, D_GROUP), lambda i, r, c: (i, r, jnp.maximum(c * hpc - 1, 0), col)),
                pl.BlockSpec((1, 1, qc, D_GROUP), lambda i, r, c: (i, r, c, col)),
                pl.BlockSpec((1, 1, RADIUS, D_GROUP), lambda i, r, c: (i, r, jnp.minimum((c + 1) * hpc, last), col))]

    out = pl.BlockSpec((1, 1, qc, D_GROUP), lambda i, r, c: (i, r, c, 0))
    return pl.pallas_call(
        functools.partial(_attn_kernel, n=n, qc=qc),
        grid=(b, dil, n // qc),
        in_specs=[halo_specs(0)[1]] + halo_specs(1) + halo_specs(2),
        out_specs=[out, out],
        out_shape=[jax.ShapeDtypeStruct((b, dil, n, D_GROUP), F32)] * 2,
        scratch_shapes=[pltpu.VMEM((2, qc, D_GROUP), BF16), pltpu.VMEM((qc + 2 * RADIUS, D_GROUP), BF16),
                        pltpu.VMEM((qc + 2 * RADIUS, D_GROUP), BF16)],
        compiler_params=_params("parallel", "parallel", "parallel"),
        name=f"dilated_attention_d{dil}",
    )(att, att, att, att, att, att, att)


RG_ROWS = 512
RG_SEG = SEQ // SUBLANES
RG_HALO = 3


def _rglru_kernel(xr_ref, gate_ref, cw_ref, cb_ref, wg_ref, bg_ref, lam_ref, o_ref,
                  xe_s, gi_s, a_s, p_s, h_s, q_s):
    grp = lambda j: pl.ds(pl.multiple_of(j * SUBLANES, SUBLANES), SUBLANES)

    def gather(j, carry):
        src = pl.ds(j, SUBLANES, stride=RG_SEG)
        xe_s[grp(j + 2), :] = xr_ref[0, src, :]
        gi_s[grp(j), :] = gate_ref[0, src, :]
        return carry

    lax.fori_loop(0, RG_SEG, gather, 0, unroll=8)

    sub = lax.broadcasted_iota(jnp.int32, (SUBLANES, LANES), 0)
    vreg = lambda j: slice((j + 2) * SUBLANES, (j + 3) * SUBLANES)
    for j in (-2, -1):
        xe_s[vreg(j), :] = jnp.where(sub == 0, 0.0, pltpu.roll(xe_s[vreg(RG_SEG + j), :], 1, axis=0))
    xe_s[vreg(RG_SEG), :] = jnp.where(sub == SUBLANES - 1, 0.0,
                                      pltpu.roll(xe_s[vreg(0), :], SUBLANES - 1, axis=0))

    lam = lam_ref[...]
    softplus = jnp.maximum(-lam, 0.0) + jnp.log1p(jnp.exp(-jnp.abs(lam)))
    log_a_scale = -RG_C * softplus
    wg = wg_ref[0]
    bg = bg_ref[0]
    cw = cw_ref[...]
    cb = cb_ref[...]

    def gates(i, carry):
        r0 = pl.multiple_of(i * RG_ROWS, RG_ROWS)
        xr = cb
        for k in range(RG_HALO + 1):
            xr = xr + cw[k:k + 1, :] * xe_s[pl.ds(r0 + k * SUBLANES, RG_ROWS), :]
        g = _dot(xr.astype(BF16), wg) + bg
        for d in range(2):
            r = 0.5 + 0.5 * jnp.tanh(0.5 * g[:, (2 * d) * LANES:(2 * d + 1) * LANES])
            ig = 0.5 + 0.5 * jnp.tanh(0.5 * g[:, (2 * d + 1) * LANES:(2 * d + 2) * LANES])
            log_a = r * log_a_scale[d:d + 1, :]
            a = jnp.exp(log_a)
            t = jnp.tanh(log_a)
            mult = jnp.sqrt(-2.0 * t / (1.0 - t))
            xn = xr * ig
            a_s[d, pl.ds(r0, RG_ROWS), :] = a
            p_s[d, pl.ds(r0, RG_ROWS), :] = xn * mult
            edge = slice(0, SUBLANES) if d == 0 else slice(RG_ROWS - SUBLANES, RG_ROWS)
            at_start = sub == (0 if d == 0 else SUBLANES - 1)

            @pl.when(i == (0 if d == 0 else SEQ // RG_ROWS - 1))
            def _():
                p_s[d, pl.ds(r0 + edge.start, SUBLANES), :] = xn[edge] * jnp.where(at_start, 1.0, mult[edge])
        return carry

    lax.fori_loop(0, SEQ // RG_ROWS, gates, 0)

    def scan(j, carry):
        hf, pf, hb, pb = carry
        idx_f = grp(j)
        idx_b = grp(RG_SEG - 1 - j)
        af = a_s[0, idx_f, :]
        hf = af * hf + p_s[0, idx_f, :]
        pf = pf * af
        h_s[0, idx_f, :] = hf
        q_s[0, idx_f, :] = pf
        ab = a_s[1, idx_b, :]
        hb = ab * hb + p_s[1, idx_b, :]
        pb = pb * ab
        h_s[1, idx_b, :] = hb
        q_s[1, idx_b, :] = pb
        return hf, pf, hb, pb

    zero = jnp.zeros((SUBLANES, LANES), F32)
    one = jnp.ones((SUBLANES, LANES), F32)
    hf, pf, hb, pb = lax.fori_loop(0, RG_SEG, scan, (zero, one, zero, one), unroll=8)

    cf = zero
    for s in range(1, SUBLANES):
        cf = jnp.where(sub == s, pltpu.roll(hf + pf * cf, 1, axis=0), cf)
    cr = zero
    for s in range(SUBLANES - 2, -1, -1):
        cr = jnp.where(sub == s, pltpu.roll(hb + pb * cr, SUBLANES - 1, axis=0), cr)

    def finish(j, carry):
        rows = grp(j)
        h = h_s[0, rows, :] + q_s[0, rows, :] * cf + h_s[1, rows, :] + q_s[1, rows, :] * cr
        o_ref[0, pl.ds(j, SUBLANES, stride=RG_SEG), :] = h * jax.nn.gelu(gi_s[rows, :])
        return carry

    lax.fori_loop(0, RG_SEG, finish, 0, unroll=8)


def _rglru_gate_weights(gate_w, gate_b):
    nchunk = D_RG // LANES
    bd = D_RG // RG_BLOCKS
    per = LANES // bd
    w = jnp.zeros((nchunk, LANES, 4, LANES), F32)
    for c in range(nchunk):
        for j in range(per):
            blk = gate_w[:, :, c * per + j].reshape(4, bd, bd)
            w = w.at[c, j * bd:(j + 1) * bd, :, j * bd:(j + 1) * bd].set(jnp.transpose(blk, (1, 0, 2)))
    w = w.reshape(nchunk, LANES, 4 * LANES).astype(BF16)
    b = gate_b.reshape(4, nchunk, LANES).transpose(1, 0, 2).reshape(nchunk, 1, 4 * LANES)
    return w, b


def _rglru_mixer(main, b, conv_w, conv_b, gate_w, gate_b, lam):
    nchunk = D_RG // LANES
    wg, bg = _rglru_gate_weights(gate_w, gate_b)
    main3 = main.reshape(b, SEQ, D_MAIN)
    blk = lambda off: pl.BlockSpec((1, SEQ, LANES), lambda i, c, off=off: (i, 0, off // LANES + c))
    yc = pl.pallas_call(
        _rglru_kernel,
        grid=(b, nchunk),
        in_specs=[blk(MAIN_RG), blk(MAIN_RG_GATE),
                  pl.BlockSpec((conv_w.shape[0], LANES), lambda i, c: (0, c)),
                  pl.BlockSpec((1, LANES), lambda i, c: (0, c)),
                  pl.BlockSpec((1, LANES, 4 * LANES), lambda i, c: (c, 0, 0)),
                  pl.BlockSpec((1, 1, 4 * LANES), lambda i, c: (c, 0, 0)),
                  pl.BlockSpec((2, LANES), lambda i, c: (0, c))],
        out_specs=pl.BlockSpec((1, SEQ, LANES), lambda i, c: (i, 0, c)),
        out_shape=jax.ShapeDtypeStruct((b, SEQ, D_RG), F32),
        scratch_shapes=[pltpu.VMEM((SEQ + RG_HALO * SUBLANES, LANES), F32), pltpu.VMEM((SEQ, LANES), F32)]
                       + [pltpu.VMEM((2, SEQ, LANES), F32)] * 4,
        compiler_params=_params("parallel", "parallel"),
        name="rglru",
    )(main3, main3, conv_w, conv_b.reshape(1, D_RG), wg, bg, lam)
    return yc.reshape(b * SEQ, D_RG)


MERGE_TM = 512


def _layer_norm(z, g, b):
    mu = jnp.mean(z, axis=-1, keepdims=True)
    zc = z - mu
    var = jnp.mean(zc * zc, axis=-1, keepdims=True)
    return zc * lax.rsqrt(var + LN_EPS) * g + b


def _time_major(ref, dil, scratch):
    if dil == 1:
        return ref[0, 0]
    tiles = D_GROUP // LANES
    for r in range(dil):
        for h in range(tiles):
            scratch[h, pl.ds(r, MERGE_TM // dil, stride=dil), :] = ref[0, r, :, h * LANES:(h + 1) * LANES]
    return jnp.concatenate([scratch[h] for h in range(tiles)], axis=1)


def _merge_kernel(x_ref, ya_ref, o0_ref, o1_ref, o2_ref, l0_ref, l1_ref, l2_ref, yc_ref,
                  wg_ref, bgate_ref, wa_ref, wb_ref, wc_ref, wo_ref, g_ref, b_ref, out_ref,
                  s0, s1, s2, s3):
    x = x_ref[...]
    xb = x.astype(BF16)
    d0, d1, d2 = DILATIONS
    o0, l0 = _time_major(o0_ref, d0, s0), _time_major(l0_ref, d0, s0)
    o1, l1 = _time_major(o1_ref, d1, s0), _time_major(l1_ref, d1, s1)
    o2, l2 = _time_major(o2_ref, d2, s2), _time_major(l2_ref, d2, s3)
    lmax = jnp.maximum(jnp.maximum(l0, l1), l2)
    e0, e1, e2 = jnp.exp(l0 - lmax), jnp.exp(l1 - lmax), jnp.exp(l2 - lmax)
    yb = (e0 * o0 + e1 * o1 + e2 * o2) / (e0 + e1 + e2)
    branches = ((ya_ref[...], wa_ref), (yb, wb_ref), (yc_ref[...], wc_ref))
    mixed = None
    for j, (y, w_ref) in enumerate(branches):
        cols = slice(j * D_MODEL, (j + 1) * D_MODEL)
        gate = jax.nn.sigmoid(_dot(xb, wg_ref[:, cols]) + bgate_ref[:, cols])
        term = gate * _dot(y.astype(BF16), w_ref[...])
        mixed = term if mixed is None else mixed + term
    z = DN_ALPHA * x + _dot(mixed.astype(BF16), wo_ref[...])
    out_ref[...] = _layer_norm(z, g_ref[...], b_ref[...])


def _merge_and_project(x2d, ya, att, yc, w_gate, b_gate, w_a, w_b, w_c, w_o, ln_g, ln_b):
    t = x2d.shape[0]
    tps = SEQ // MERGE_TM
    rows = lambda w: pl.BlockSpec((MERGE_TM, w), lambda i: (i, 0))
    grp = lambda d: pl.BlockSpec((1, d, MERGE_TM // d, D_GROUP), lambda i: (i // tps, 0, i % tps, 0))
    const = lambda a: pl.BlockSpec(a.shape, lambda i: (0,) * a.ndim, pipeline_mode=pl.Buffered(1))
    (o0, l0), (o1, l1), (o2, l2) = att
    weights = (w_gate, b_gate.reshape(1, -1), w_a, w_b, w_c, w_o, ln_g.reshape(1, -1), ln_b.reshape(1, -1))
    return pl.pallas_call(
        _merge_kernel,
        grid=(t // MERGE_TM,),
        in_specs=[rows(D_MODEL), rows(D_HY)] + [grp(d) for d in DILATIONS] * 2 + [rows(D_RG)]
                 + [const(a) for a in weights],
        out_specs=rows(D_MODEL),
        out_shape=jax.ShapeDtypeStruct((t, D_MODEL), F32),
        scratch_shapes=[pltpu.VMEM((D_GROUP // LANES, MERGE_TM, LANES), F32)] * 4,
        compiler_params=_params("parallel"),
        name="merge_project_norm",
    )(x2d, ya, o0, o1, o2, l0, l1, l2, yc, *weights)


FFN_TM = 512
FFN_CF = 512


def _ffn_kernel(x_ref, prev_ref, next_ref, wup_ref, cw_ref, cb_ref, wdn_ref, g_ref, b_ref, out_ref):
    i = pl.program_id(0)
    tiles_per_seq = SEQ // FFN_TM
    x = x_ref[...]
    xb = x.astype(BF16)
    has_prev = (i % tiles_per_seq) != 0
    has_next = (i % tiles_per_seq) != tiles_per_seq - 1
    halo = jnp.concatenate([prev_ref[...], next_ref[...]], axis=0).astype(BF16)
    xcat = jnp.concatenate([xb, halo], axis=0)
    row = lax.broadcasted_iota(jnp.int32, (FFN_TM, FFN_CF), 0)
    acc = None
    for c in range(D_FF // FFN_CF):
        cols = slice(c * FFN_CF, (c + 1) * FFN_CF)
        ucols = slice(D_FF + c * FFN_CF, D_FF + (c + 1) * FFN_CF)
        hgx = _dot(xcat, wup_ref[:, cols])
        hg = hgx[:FFN_TM]
        before = jnp.where(has_prev, hgx[FFN_TM + SUBLANES - 1:FFN_TM + SUBLANES, :], 0.0)
        after = jnp.where(has_next, hgx[FFN_TM + SUBLANES:FFN_TM + SUBLANES + 1, :], 0.0)
        up = jnp.where(row == 0, before, pltpu.roll(hg, 1, axis=0))
        dn = jnp.where(row == FFN_TM - 1, after, pltpu.roll(hg, FFN_TM - 1, axis=0))
        w = cw_ref[:, cols]
        conv = up * w[0:1, :] + hg * w[1:2, :] + dn * w[2:3, :] + cb_ref[:, cols]
        act = jax.nn.gelu(conv) * _dot(xb, wup_ref[:, ucols])
        term = _dot(act.astype(BF16), wdn_ref[cols, :])
        acc = term if acc is None else acc + term
    out_ref[...] = _layer_norm(DN_ALPHA * x + acc, g_ref[...], b_ref[...])


def _ffn(x2d, w_up, conv_w, conv_b, w_down, ln_g, ln_b):
    t = x2d.shape[0]
    bpt = FFN_TM // SUBLANES
    nblk = t // SUBLANES
    const = lambda a: pl.BlockSpec(a.shape, lambda i: (0,) * a.ndim, pipeline_mode=pl.Buffered(1))
    weights = (w_up, conv_w, conv_b.reshape(1, -1), w_down, ln_g.reshape(1, -1), ln_b.reshape(1, -1))
    return pl.pallas_call(
        _ffn_kernel,
        grid=(t // FFN_TM,),
        in_specs=[pl.BlockSpec((FFN_TM, D_MODEL), lambda i: (i, 0)),
                  pl.BlockSpec((SUBLANES, D_MODEL), lambda i: (jnp.maximum(i * bpt - 1, 0), 0)),
                  pl.BlockSpec((SUBLANES, D_MODEL), lambda i: (jnp.minimum((i + 1) * bpt, nblk - 1), 0))]
                 + [const(a) for a in weights],
        out_specs=pl.BlockSpec((FFN_TM, D_MODEL), lambda i: (i, 0)),
        out_shape=jax.ShapeDtypeStruct((t, D_MODEL), F32),
        compiler_params=_params("parallel"),
        name="ffn_norm",
    )(x2d, x2d, x2d, *weights)


def _encoder_layer(x2d, b, l, p, khat):
    main, *att = _input_projection(x2d, p["w_in"][l])
    ya = _hyena_mixer(main, b, p["hy_conv_w"][l], p["hy_conv_b"][l], khat[l], p["hy_bias"][l])
    att_out = [_dilated_attention_group(a, dil) for a, dil in zip(att, DILATIONS)]
    yc = _rglru_mixer(main, b, p["rg_conv_w"][l], p["rg_conv_b"][l], p["rg_gate_w"][l], p["rg_gate_b"][l],
                      p["rg_lam"][l])
    x2d = _merge_and_project(x2d, ya, att_out, yc,
                             p["w_gate"][l], p["b_gate"][l], p["w_br_a"][l], p["w_br_b"][l], p["w_br_c"][l],
                             p["w_o"][l], p["ln1_g"][l], p["ln1_b"][l])
    return _ffn(x2d, p["w_up"][l], p["ffn_conv_w"][l], p["ffn_conv_b"][l], p["w_down"][l],
                p["ln2_g"][l], p["ln2_b"][l])


def _trunk(x, p):
    b = x.shape[0]
    khat = _hyena_filter_spectra(p["hy_filt_w1"], p["hy_filt_b1"], p["hy_filt_w2"], p["hy_filt_b2"],
                                 p["hy_filt_w3"], p["hy_filt_b3"], p["hy_filt_freq"])
    x2d = x.reshape(b * SEQ, D_MODEL)
    for l in range(DEPTH):
        x2d = _encoder_layer(x2d, b, l, p, khat)
    return x2d.reshape(b, SEQ, D_MODEL)


_MATMUL_WEIGHTS = ("w_in", "w_gate", "w_br_a", "w_br_b", "w_br_c", "w_o", "w_up", "w_down")


def kernel(x_prompt, x_sample, w_in, hy_conv_w, hy_conv_b, hy_filt_w1, hy_filt_b1, hy_filt_w2, hy_filt_b2, hy_filt_w3, hy_filt_b3, hy_filt_freq, hy_bias, rg_conv_w, rg_conv_b, rg_gate_w, rg_gate_b, rg_lam, w_gate, b_gate, w_br_a, w_br_b, w_br_c, w_o, ln1_g, ln1_b, w_up, ffn_conv_w, ffn_conv_b, w_down, ln2_g, ln2_b):
    p = dict(w_in=w_in, hy_conv_w=hy_conv_w, hy_conv_b=hy_conv_b, hy_filt_w1=hy_filt_w1,
             hy_filt_b1=hy_filt_b1, hy_filt_w2=hy_filt_w2, hy_filt_b2=hy_filt_b2, hy_filt_w3=hy_filt_w3,
             hy_filt_b3=hy_filt_b3, hy_filt_freq=hy_filt_freq, hy_bias=hy_bias, rg_conv_w=rg_conv_w,
             rg_conv_b=rg_conv_b, rg_gate_w=rg_gate_w, rg_gate_b=rg_gate_b, rg_lam=rg_lam, w_gate=w_gate,
             b_gate=b_gate, w_br_a=w_br_a, w_br_b=w_br_b, w_br_c=w_br_c, w_o=w_o, ln1_g=ln1_g, ln1_b=ln1_b,
             w_up=w_up, ffn_conv_w=ffn_conv_w, ffn_conv_b=ffn_conv_b, w_down=w_down, ln2_g=ln2_g,
             ln2_b=ln2_b)
    for name in _MATMUL_WEIGHTS:
        p[name] = p[name].astype(BF16)
    nb = x_prompt.shape[0]
    y = _trunk(jnp.concatenate([x_prompt, x_sample], axis=0), p)
    return (y[:nb], y[nb:])
```

```python
import functools
import math

import numpy as np
import jax
import jax.numpy as jnp
from jax import lax
from jax.experimental import pallas as pl
from jax.experimental.pallas import tpu as pltpu

F32 = jnp.float32
BF16 = jnp.bfloat16

D_MODEL = 1024
SEQ = 4096
DEPTH = 2
D_HY = 512
HY_EMB_BANDS = 8
HY_EMB_PAD = 32
HY_FAST_DECAY = 0.3
HY_SLOW_DECAY = 1.5
HY_TARGET = 1e-2
HEAD_DIM = 64
HEADS_PER_GROUP = 4
DILATIONS = (1, 4, 16)
RADIUS = 64
D_GROUP = HEADS_PER_GROUP * HEAD_DIM
D_ATT = len(DILATIONS) * D_GROUP
ROPE_THETA = 10000.0
NEG_INF = -1e30
DEN_FLOOR = 1e-30
D_RG = 512
RG_BLOCKS = 8
RG_C = 8.0
D_FF = 3 * D_MODEL
DN_ALPHA = (2 * DEPTH) ** 0.25
LN_EPS = 1e-5
D_IN = 3 * D_HY + 3 * D_ATT + 2 * D_RG
OFF_Q = 3 * D_HY
OFF_K = OFF_Q + D_ATT
OFF_V = OFF_K + D_ATT
OFF_RG = OFF_V + D_ATT
D_MAIN = 3 * D_HY + 2 * D_RG
MAIN_RG = 3 * D_HY
MAIN_RG_GATE = MAIN_RG + D_RG

LANES = 128
SUBLANES = 8
VMEM_LIMIT = 56 * 1024 * 1024

FFT_N = 2 * SEQ
FFT_N1 = 64
FFT_N2 = 128
FFT_HALF = FFT_N1 // 2
HY_COLS = FFT_N2 * D_HY
HY_SLABS = FFT_N1 // 2 + 1
HY_SLABS_PAD = 40


def _params(*sem):
    return pltpu.CompilerParams(dimension_semantics=sem, vmem_limit_bytes=VMEM_LIMIT)


def _dot(a, b):
    return jnp.dot(a, b, preferred_element_type=F32)


def _cos_sin(a, b, period):
    ang = 2.0 * np.pi * (np.outer(a, b) % period) / period
    return np.cos(ang), np.sin(ang)


@functools.lru_cache(maxsize=None)
def _dft_constants():
    k1 = np.arange(FFT_N1)
    c, s = _cos_sin(k1, np.arange(FFT_N1), FFT_N1)
    f1_full = np.concatenate([c, -s], axis=0)
    n2 = np.arange(FFT_N2)
    k2 = np.arange(FFT_N2)
    g = np.zeros((FFT_N1, 2 * FFT_N2, 2 * FFT_N2), np.float64)
    for a in range(FFT_N1):
        c, s = _cos_sin(a + FFT_N1 * k2, n2, FFT_N)
        gr, gi = c, -s
        g[a] = np.block([[gr, -gi], [gi, gr]])
    return jnp.asarray(f1_full, BF16), jnp.asarray(g, BF16)


@functools.lru_cache(maxsize=None)
def _hyena_stage_constants():
    k1 = np.arange(HY_SLABS_PAD)
    n1 = np.arange(FFT_HALF)
    eye = np.eye(SUBLANES)
    c, s = _cos_sin(k1, n1, FFT_N1)
    m1 = np.kron(np.concatenate([c, -s], axis=0), eye)
    wgt = np.where((k1 == 0) | (k1 == FFT_N1 // 2), 1.0, 2.0) * (k1 < HY_SLABS)
    c, s = _cos_sin(n1, k1, FFT_N1)
    m3 = np.kron(np.concatenate([c * wgt, -s * wgt], axis=1) / FFT_N, eye)
    return jnp.asarray(m1, BF16), jnp.asarray(m3, BF16)


@functools.lru_cache(maxsize=None)
def _filter_positions():
    L = SEQ
    t = np.linspace(0.0, 1.0, L, dtype=np.float32).astype(np.float64)[:, None]
    w = (2.0 * np.pi * np.arange(L, dtype=np.float32) / L).astype(np.float64)[:, None]
    bands = np.linspace(1e-4, HY_EMB_BANDS - 1, HY_EMB_BANDS, dtype=np.float32).astype(np.float64)[None, :]
    z = np.concatenate([t, np.cos(bands * w), -np.sin(bands * w)], axis=-1)
    m = np.arange(2 * L)
    src = np.where(m < L, m, 2 * L - m)
    src = np.where(m == L, 0, src)
    zp = np.zeros((2 * L, HY_EMB_PAD))
    zp[:, :z.shape[1]] = z[src]
    return jnp.asarray(zp, F32)


@functools.lru_cache(maxsize=None)
def _rope_tables():
    inv = ROPE_THETA ** (-np.arange(0, HEAD_DIM, 2, dtype=np.float32).astype(np.float64) / HEAD_DIM)
    ang = np.arange(SEQ, dtype=np.float64)[:, None] * inv[None, :]
    cos = np.concatenate([np.cos(ang), np.cos(ang)], axis=1)
    sin = np.concatenate([-np.sin(ang), np.sin(ang)], axis=1)
    cos = np.tile(cos, (1, HEADS_PER_GROUP))
    sin = np.tile(sin, (1, HEADS_PER_GROUP))
    return jnp.asarray(cos, F32), jnp.asarray(sin, F32)


PROJ_TM = 512
_MAIN_CHUNKS = tuple((src, dst, 512) for src, dst in
                     [(j, j) for j in range(0, 3 * D_HY, 512)] +
                     [(OFF_RG + j, MAIN_RG + j) for j in range(0, 2 * D_RG, 512)])


def _rope(t, cos, sin):
    lane = lax.broadcasted_iota(jnp.int32, t.shape, 1)
    first_half = (lane % HEAD_DIM) < (HEAD_DIM // 2)
    rot = jnp.where(first_half, pltpu.roll(t, D_GROUP - HEAD_DIM // 2, axis=1),
                    pltpu.roll(t, HEAD_DIM // 2, axis=1))
    return t * cos + rot * sin


def _proj_kernel(x_ref, w_ref, cos_ref, sin_ref, main_ref, a0_ref, a1_ref, a2_ref, qkv_s):
    xb = x_ref[...].astype(BF16)
    for src, dst, width in _MAIN_CHUNKS:
        main_ref[:, dst:dst + width] = _dot(xb, w_ref[:, src:src + width])
    cos = cos_ref[...]
    sin = sin_ref[...]
    for g, (dil, att_ref) in enumerate(zip(DILATIONS, (a0_ref, a1_ref, a2_ref))):
        col = lambda off: slice(off + g * D_GROUP, off + (g + 1) * D_GROUP)
        q = _rope(_dot(xb, w_ref[:, col(OFF_Q)]), cos, sin) * (HEAD_DIM ** -0.5)
        k = _rope(_dot(xb, w_ref[:, col(OFF_K)]), cos, sin)
        v = _dot(xb, w_ref[:, col(OFF_V)])
        if dil == 1:
            att_ref[0, 0, :, 0:D_GROUP] = q
            att_ref[0, 0, :, D_GROUP:2 * D_GROUP] = k
            att_ref[0, 0, :, 2 * D_GROUP:] = v
        else:
            for j, t in enumerate((q, k, v)):
                for h in range(D_GROUP // LANES):
                    qkv_s[j * (D_GROUP // LANES) + h] = t[:, h * LANES:(h + 1) * LANES]
            for r in range(dil):
                for j in range(3 * D_GROUP // LANES):
                    att_ref[0, r, :, j * LANES:(j + 1) * LANES] = qkv_s[j, pl.ds(r, PROJ_TM // dil, stride=dil), :]


def _input_projection(x2d, w_in_bf16):
    t = x2d.shape[0]
    b = t // SEQ
    tps = SEQ // PROJ_TM
    cos, sin = _rope_tables()
    tab = pl.BlockSpec((PROJ_TM, D_GROUP), lambda i: (i % tps, 0))
    att_spec = lambda d: pl.BlockSpec((1, d, PROJ_TM // d, 3 * D_GROUP), lambda i: (i // tps, 0, i % tps, 0))
    att_shape = lambda d: jax.ShapeDtypeStruct((b, d, SEQ // d, 3 * D_GROUP), F32)
    return pl.pallas_call(
        _proj_kernel,
        grid=(t // PROJ_TM,),
        in_specs=[pl.BlockSpec((PROJ_TM, D_MODEL), lambda i: (i, 0)),
                  pl.BlockSpec((D_MODEL, D_IN), lambda i: (0, 0), pipeline_mode=pl.Buffered(1)),
                  tab, tab],
        out_specs=[pl.BlockSpec((PROJ_TM, D_MAIN), lambda i: (i, 0))] + [att_spec(d) for d in DILATIONS],
        out_shape=[jax.ShapeDtypeStruct((t, D_MAIN), F32)] + [att_shape(d) for d in DILATIONS],
        scratch_shapes=[pltpu.VMEM((3 * D_GROUP // LANES, PROJ_TM, LANES), F32)],
        compiler_params=_params("parallel"),
        name="input_projection",
    )(x2d, w_in_bf16, cos, sin)


def _conv_rows(x, w, bias, pad_left):
    n = x.shape[0]
    row = lax.broadcasted_iota(jnp.int32, x.shape, 0)
    acc = None
    for k in range(w.shape[0]):
        off = k - pad_left
        if off == 0:
            xs = x
        else:
            xs = pltpu.roll(x, (-off) % n, axis=0)
            valid = (row >= -off) if off < 0 else (row < n - off)
            xs = jnp.where(valid, xs, 0.0)
        term = xs * w[k:k + 1, :]
        acc = term if acc is None else acc + term
    return acc + bias


FILT_ROWS = 512


def _hy_filter_kernel(z_ref, w1_ref, b1_ref, w2_ref, b2_ref, w3f_ref, w3b_ref, b3f_ref, b3b_ref,
                      fr_ref, dl_ref, k_ref, h_s):
    hi = lax.Precision.HIGHEST
    w3f, w3b, b3f, b3b = w3f_ref[0], w3b_ref[0], b3f_ref[0], b3b_ref[0]
    delta = dl_ref[...]

    @pl.when(pl.program_id(1) == 0)
    def _():
        fr = fr_ref[0]
        w1, b1, w2, b2 = w1_ref[0], b1_ref[0], w2_ref[0], b2_ref[0]

        def hidden(i, carry):
            rows = pl.ds(pl.multiple_of(i * FILT_ROWS, FILT_ROWS), FILT_ROWS)
            h = jnp.sin(fr * (jnp.dot(z_ref[rows, :], w1, precision=hi, preferred_element_type=F32) + b1))
            h_s[rows, :] = jnp.sin(fr * (jnp.dot(h, w2, precision=hi, preferred_element_type=F32) + b2))
            return carry

        lax.fori_loop(0, FFT_N // FILT_ROWS, hidden, 0)

    def body(i, total):
        r0 = pl.multiple_of(i * FILT_ROWS, FILT_ROWS)
        z = z_ref[pl.ds(r0, FILT_ROWS), :]
        h = h_s[pl.ds(r0, FILT_ROWS), :]
        hf = jnp.dot(h, w3f, precision=hi, preferred_element_type=F32) + b3f
        hb = jnp.dot(h, w3b, precision=hi, preferred_element_type=F32) + b3b
        m = lax.broadcasted_iota(jnp.int32, hf.shape, 0) + r0
        decay = jnp.exp(-z[:, 0:1] * delta)
        val = jnp.where(m < SEQ, hf, hb) * decay
        val = jnp.where(m == SEQ, 0.0, val)
        k_ref[0, pl.ds(r0, FILT_ROWS), :] = val
        return total + jnp.sum(jnp.abs(val), axis=0, keepdims=True)

    total = lax.fori_loop(0, FFT_N // FILT_ROWS, body, jnp.zeros((1, LANES), F32))

    def scale(i, carry):
        r0 = pl.multiple_of(i * FILT_ROWS, FILT_ROWS)
        k_ref[0, pl.ds(r0, FILT_ROWS), :] = k_ref[0, pl.ds(r0, FILT_ROWS), :] / total
        return carry

    lax.fori_loop(0, FFT_N // FILT_ROWS, scale, 0)


def _hyena_filters(w1, b1, w2, b2, w3, b3, freq):
    nchunk = D_HY // LANES
    z = _filter_positions()
    emb = z.shape[1]
    w1 = jnp.pad(w1, ((0, 0), (0, emb - w1.shape[1]), (0, 0)))
    hid = w1.shape[-1]
    deltas = np.abs(np.linspace(math.log(HY_TARGET) / HY_SLOW_DECAY, math.log(HY_TARGET) / HY_FAST_DECAY,
                                D_HY, dtype=np.float32))
    deltas = jnp.asarray(deltas, F32).reshape(1, D_HY)
    row = lambda a: a.reshape(DEPTH, 1, a.shape[-1])
    full = lambda s: pl.BlockSpec((1,) + s, lambda l, c: (l, 0, 0))
    return pl.pallas_call(
        _hy_filter_kernel,
        grid=(DEPTH, nchunk),
        in_specs=[pl.BlockSpec((FFT_N, emb), lambda l, c: (0, 0)),
                  full((emb, hid)), full((1, hid)), full((hid, hid)), full((1, hid)),
                  pl.BlockSpec((1, hid, LANES), lambda l, c: (l, 0, c)),
                  pl.BlockSpec((1, hid, LANES), lambda l, c: (l, 0, nchunk + c)),
                  pl.BlockSpec((1, 1, LANES), lambda l, c: (l, 0, c)),
                  pl.BlockSpec((1, 1, LANES), lambda l, c: (l, 0, nchunk + c)),
                  full((1, hid)),
                  pl.BlockSpec((1, LANES), lambda l, c: (0, c))],
        out_specs=pl.BlockSpec((1, FFT_N, LANES), lambda l, c: (l, 0, c)),
        out_shape=jax.ShapeDtypeStruct((DEPTH, FFT_N, D_HY), F32),
        scratch_shapes=[pltpu.VMEM((FFT_N, hid), F32)],
        compiler_params=_params("parallel", "arbitrary"),
        name="hyena_filter",
    )(z, w1, row(b1), w2, row(b2), w3, w3, row(b3), row(b3), row(freq), deltas)


DFT_TN = 8192


def _dft_rows_kernel(f_ref, x_ref, o_ref):
    o_ref[0] = _dot(f_ref[...], x_ref[0].astype(BF16))


def _dft_over_n1(x_view, f):
    b, k, cols = x_view.shape
    m = f.shape[0]
    return pl.pallas_call(
        _dft_rows_kernel,
        grid=(b, cols // DFT_TN),
        in_specs=[pl.BlockSpec((m, k), lambda i, j: (0, 0)),
                  pl.BlockSpec((1, k, DFT_TN), lambda i, j: (i, 0, j))],
        out_specs=pl.BlockSpec((1, m, DFT_TN), lambda i, j: (i, 0, j)),
        out_shape=jax.ShapeDtypeStruct((b, m, cols), F32),
        compiler_params=_params("parallel", "parallel"),
        name="hyena_dft_n1",
    )(f, x_view)


def _filter_spectrum_kernel(g_ref, a_ref, o_ref):
    a = a_ref[0].reshape(2 * FFT_N2, D_HY).astype(BF16)
    o_ref[0] = _dot(g_ref[0], a).reshape(2, FFT_N2, D_HY)


def _filter_spectrum(a_k, g):
    slab = pl.BlockSpec((1, 2, FFT_N2, D_HY), lambda k1, l: (l, 0, k1, 0))
    return pl.pallas_call(
        _filter_spectrum_kernel,
        grid=(FFT_N1, DEPTH),
        in_specs=[pl.BlockSpec((1, 2 * FFT_N2, 2 * FFT_N2), lambda k1, l: (k1, 0, 0)), slab],
        out_specs=slab,
        out_shape=jax.ShapeDtypeStruct(a_k.shape, F32),
        compiler_params=_params("parallel", "parallel"),
        name="hyena_filter_spectrum",
    )(g, a_k)


def _hyena_filter_spectra(w1, b1, w2, b2, w3, b3, freq):
    f1_full, g = _dft_constants()
    k = _hyena_filters(w1, b1, w2, b2, w3, b3, freq)
    a_k = _dft_over_n1(k.reshape(DEPTH, FFT_N1, HY_COLS), f1_full)
    return _filter_spectrum(a_k.reshape(DEPTH, 2, FFT_N, D_HY), g)


HY_SLAB_UNROLL = 11
HY_EDGE_UNROLL = 4


def _hyena_kernel(x0_ref, x1_ref, v_ref, cw_ref, cb_ref, m1_ref, m3_ref, g_ref, kh_ref, bias_ref, o_ref,
                  uu_s, x0_s, a_s):
    shape3 = (FFT_HALF, FFT_N2, LANES)
    nat = lambda ref: ref[0].reshape(SEQ, LANES)
    x0_s[...] = _conv_rows(nat(x0_ref), cw_ref[:, 0, :], cb_ref[0], 1).reshape(shape3)
    x1 = _conv_rows(nat(x1_ref), cw_ref[:, 1, :], cb_ref[1], 1)
    v = _conv_rows(nat(v_ref), cw_ref[:, 2, :], cb_ref[2], 1)
    uu_s[...] = (v * x1).reshape(shape3)

    def stage1(j, carry):
        blocks = [pl.ds(pl.multiple_of((j * HY_EDGE_UNROLL + u) * SUBLANES, SUBLANES), SUBLANES)
                  for u in range(HY_EDGE_UNROLL)]
        xs = [uu_s[:, n2, :].reshape(FFT_HALF * SUBLANES, LANES).astype(BF16) for n2 in blocks]
        res = [_dot(m1_ref[...], xb).reshape(2 * HY_SLABS_PAD, SUBLANES, LANES) for xb in xs]
        for n2, a in zip(blocks, res):
            a_s[0, :, n2, :] = a[:HY_SLABS_PAD]
            a_s[1, :, n2, :] = a[HY_SLABS_PAD:]
        return carry

    lax.fori_loop(0, FFT_N2 // SUBLANES // HY_EDGE_UNROLL, stage1, 0)

    def stage2(j, carry):
        slabs = [j * HY_SLAB_UNROLL + u for u in range(HY_SLAB_UNROLL)]
        gs = [g_ref[k1] for k1 in slabs]
        xs = [jnp.concatenate([a_s[0, k1], a_s[1, k1]], axis=0).astype(BF16) for k1 in slabs]
        res = []
        for k1, g, a in zip(slabs, gs, xs):
            x = _dot(g, a)
            xr, xi = x[:FFT_N2], x[FFT_N2:]
            rows = pl.ds(pl.multiple_of(k1 * FFT_N2, FFT_N2), FFT_N2)
            kr, ki = kh_ref[0, rows, :], kh_ref[1, rows, :]
            z = jnp.concatenate([xr * kr - xi * ki, xr * ki + xi * kr], axis=0).astype(BF16)
            res.append(lax.dot_general(g, z, (((0,), (0,)), ((), ())), preferred_element_type=F32))
        for k1, d in zip(slabs, res):
            a_s[0, k1] = d[:FFT_N2]
            a_s[1, k1] = d[FFT_N2:]
        return carry

    lax.fori_loop(0, HY_SLABS // HY_SLAB_UNROLL, stage2, 0)

    bias = bias_ref[...]

    def stage3(j, carry):
        blocks = [pl.ds(pl.multiple_of((j * HY_EDGE_UNROLL + u) * SUBLANES, SUBLANES), SUBLANES)
                  for u in range(HY_EDGE_UNROLL)]
        ds = [jnp.concatenate([a_s[0, :, n2, :], a_s[1, :, n2, :]], axis=0)
              .reshape(2 * HY_SLABS_PAD * SUBLANES, LANES).astype(BF16) for n2 in blocks]
        ys = [_dot(m3_ref[...], d).reshape(FFT_HALF, SUBLANES, LANES) for d in ds]
        for n2, y in zip(blocks, ys):
            o_ref[0, :, n2, :] = x0_s[:, n2, :] * (y + uu_s[:, n2, :] * bias)
        return carry

    lax.fori_loop(0, FFT_N2 // SUBLANES // HY_EDGE_UNROLL, stage3, 0)


def _hyena_mixer(main, b, conv_w, conv_b, khat, hy_bias):
    nchunk = D_HY // LANES
    m1, m3 = _hyena_stage_constants()
    _, g = _dft_constants()
    main4 = main.reshape(b, FFT_HALF, FFT_N2, D_MAIN)
    blk = lambda part: pl.BlockSpec((1, FFT_HALF, FFT_N2, LANES),
                                    lambda c, i, part=part: (i, 0, 0, part * nchunk + c))
    const = lambda a: pl.BlockSpec(a.shape, lambda c, i: (0,) * a.ndim, pipeline_mode=pl.Buffered(1))
    rows = HY_SLABS_PAD * FFT_N2
    ya = pl.pallas_call(
        _hyena_kernel,
        grid=(nchunk, b),
        in_specs=[blk(0), blk(1), blk(2),
                  pl.BlockSpec((3, 3, LANES), lambda c, i: (0, 0, c)),
                  pl.BlockSpec((3, 1, LANES), lambda c, i: (0, 0, c)),
                  const(m1), const(m3),
                  pl.BlockSpec((HY_SLABS_PAD, 2 * FFT_N2, 2 * FFT_N2), lambda c, i: (0, 0, 0),
                               pipeline_mode=pl.Buffered(1)),
                  pl.BlockSpec((2, rows, LANES), lambda c, i: (0, 0, c), pipeline_mode=pl.Buffered(1)),
                  pl.BlockSpec((1, LANES), lambda c, i: (0, c))],
        out_specs=pl.BlockSpec((1, FFT_HALF, FFT_N2, LANES), lambda c, i: (i, 0, 0, c)),
        out_shape=jax.ShapeDtypeStruct((b, FFT_HALF, FFT_N2, D_HY), F32),
        scratch_shapes=[pltpu.VMEM((FFT_HALF, FFT_N2, LANES), F32),
                        pltpu.VMEM((FFT_HALF, FFT_N2, LANES), F32),
                        pltpu.VMEM((2, HY_SLABS_PAD, FFT_N2, LANES), F32)],
        compiler_params=_params("parallel", "parallel"),
        name="hyena",
    )(main4, main4, main4, conv_w.reshape(3, 3, D_HY), conv_b.reshape(3, 1, D_HY), m1, m3, g, khat,
      hy_bias.reshape(1, D_HY))
    return ya.reshape(b * SEQ, D_HY)


ATT_TQ = 128
ATT_WIN = ATT_TQ + 2 * RADIUS
ATT_QC = 1024
ATT_UNROLL = 4


def _attn_kernel(q_ref, kp_ref, k_ref, kn_ref, vp_ref, v_ref, vn_ref, o_ref, lse_ref, qs, ks, vs, *, n, qc):
    c0 = pl.program_id(2) * qc
    q = q_ref[0, 0]
    lane = lax.broadcasted_iota(jnp.int32, (qc, D_GROUP), 1)
    even_head = (lane % LANES) < HEAD_DIM
    qs[0] = jnp.where(even_head, q, 0.0).astype(BF16)
    qs[1] = jnp.where(even_head, 0.0, q).astype(BF16)
    ks[0:RADIUS] = kp_ref[0, 0].astype(BF16)
    ks[RADIUS:RADIUS + qc] = k_ref[0, 0].astype(BF16)
    ks[RADIUS + qc:] = kn_ref[0, 0].astype(BF16)
    vs[0:RADIUS] = vp_ref[0, 0].astype(BF16)
    vs[RADIUS:RADIUS + qc] = v_ref[0, 0].astype(BF16)
    vs[RADIUS + qc:] = vn_ref[0, 0].astype(BF16)

    lane_p = lax.broadcasted_iota(jnp.int32, (ATT_TQ, LANES), 1)
    low_head = lane_p < HEAD_DIM
    row = lax.broadcasted_iota(jnp.int32, (ATT_TQ, ATT_WIN), 0)
    col = lax.broadcasted_iota(jnp.int32, (ATT_TQ, ATT_WIN), 1)
    band = jnp.abs(col - RADIUS - row) - RADIUS

    def query_block(q0):
        pos = col + (c0 + q0 - RADIUS)
        mask = jnp.maximum(jnp.maximum(band, -pos), pos - (n - 1)) <= 0
        for hp in range(D_GROUP // LANES):
            ls = slice(hp * LANES, (hp + 1) * LANES)
            kp = ks[pl.ds(q0, ATT_WIN), ls]
            vp = vs[pl.ds(q0, ATT_WIN), ls]
            outs, lses = [], []
            for h in range(2):
                s = lax.dot_general(qs[h, pl.ds(q0, ATT_TQ), ls], kp, (((1,), (1,)), ((), ())),
                                    preferred_element_type=F32)
                s = jnp.where(mask, s, NEG_INF)
                m = jnp.max(s, axis=-1, keepdims=True)
                e = jnp.where(mask, jnp.exp(s - m), 0.0)
                den = jnp.maximum(jnp.sum(e, axis=-1, keepdims=True), DEN_FLOOR)
                outs.append(_dot(e.astype(BF16), vp) / den)
                lses.append(m + jnp.log(den))
            o_ref[0, 0, pl.ds(q0, ATT_TQ), ls] = jnp.where(low_head, outs[0], outs[1])
            lse_ref[0, 0, pl.ds(q0, ATT_TQ), ls] = jnp.where(low_head, lses[0], lses[1])

    unroll = min(ATT_UNROLL, qc // ATT_TQ)

    def body(i, carry):
        for u in range(unroll):
            query_block(pl.multiple_of((i * unroll + u) * ATT_TQ, ATT_TQ))
        return carry

    lax.fori_loop(0, qc // ATT_TQ // unroll, body, 0)


def _dilated_attention_group(att, dil):
    b = att.shape[0]
    n = SEQ // dil
    qc = min(n, ATT_QC)
    hpc = qc // RADIUS
    last = n // RADIUS - 1

    def halo_specs(col):
        return [pl.BlockSpec((1, 1, RADIUS, D_GROUP), lambda i, r, c: (i, r, jnp.maximum(c * hpc - 1, 0), col)),
                pl.BlockSpec((1, 1, qc, D_GROUP), lambda i, r, c: (i, r, c, col)),
                pl.BlockSpec((1, 1, RADIUS, D_GROUP), lambda i, r, c: (i, r, jnp.minimum((c + 1) * hpc, last), col))]

    out = pl.BlockSpec((1, 1, qc, D_GROUP), lambda i, r, c: (i, r, c, 0))
    return pl.pallas_call(
        functools.partial(_attn_kernel, n=n, qc=qc),
        grid=(b, dil, n // qc),
        in_specs=[halo_specs(0)[1]] + halo_specs(1) + halo_specs(2),
        out_specs=[out, out],
        out_shape=[jax.ShapeDtypeStruct((b, dil, n, D_GROUP), F32)] * 2,
        scratch_shapes=[pltpu.VMEM((2, qc, D_GROUP), BF16), pltpu.VMEM((qc + 2 * RADIUS, D_GROUP), BF16),
                        pltpu.VMEM((qc + 2 * RADIUS, D_GROUP), BF16)],
        compiler_params=_params("parallel", "parallel", "parallel"),
        name=f"dilated_attention_d{dil}",
    )(att, att, att, att, att, att, att)


RG_ROWS = 512
RG_SEG = SEQ // SUBLANES
RG_HALO = 3


def _rglru_kernel(xr_ref, gate_ref, cw_ref, cb_ref, wg_ref, bg_ref, lam_ref, o_ref,
                  xe_s, gi_s, a_s, p_s, h_s, q_s):
    grp = lambda j: pl.ds(pl.multiple_of(j * SUBLANES, SUBLANES), SUBLANES)

    def gather(j, carry):
        src = pl.ds(j, SUBLANES, stride=RG_SEG)
        xe_s[grp(j + 2), :] = xr_ref[0, src, :]
        gi_s[grp(j), :] = gate_ref[0, src, :]
        return carry

    lax.fori_loop(0, RG_SEG, gather, 0, unroll=8)

    sub = lax.broadcasted_iota(jnp.int32, (SUBLANES, LANES), 0)
    vreg = lambda j: slice((j + 2) * SUBLANES, (j + 3) * SUBLANES)
    for j in (-2, -1):
        xe_s[vreg(j), :] = jnp.where(sub == 0, 0.0, pltpu.roll(xe_s[vreg(RG_SEG + j), :], 1, axis=0))
    xe_s[vreg(RG_SEG), :] = jnp.where(sub == SUBLANES - 1, 0.0,
                                      pltpu.roll(xe_s[vreg(0), :], SUBLANES - 1, axis=0))

    lam = lam_ref[...]
    softplus = jnp.maximum(-lam, 0.0) + jnp.log1p(jnp.exp(-jnp.abs(lam)))
    log_a_scale = -RG_C * softplus
    wg = wg_ref[0]
    bg = bg_ref[0]
    cw = cw_ref[...]
    cb = cb_ref[...]

    def gates(i, carry):
        r0 = pl.multiple_of(i * RG_ROWS, RG_ROWS)
        xr = cb
        for k in range(RG_HALO + 1):
            xr = xr + cw[k:k + 1, :] * xe_s[pl.ds(r0 + k * SUBLANES, RG_ROWS), :]
        g = _dot(xr.astype(BF16), wg) + bg
        for d in range(2):
            r = 0.5 + 0.5 * jnp.tanh(0.5 * g[:, (2 * d) * LANES:(2 * d + 1) * LANES])
            ig = 0.5 + 0.5 * jnp.tanh(0.5 * g[:, (2 * d + 1) * LANES:(2 * d + 2) * LANES])
            log_a = r * log_a_scale[d:d + 1, :]
            a = jnp.exp(log_a)
            t = jnp.tanh(log_a)
            mult = jnp.sqrt(-2.0 * t / (1.0 - t))
            xn = xr * ig
            a_s[d, pl.ds(r0, RG_ROWS), :] = a
            p_s[d, pl.ds(r0, RG_ROWS), :] = xn * mult
            edge = slice(0, SUBLANES) if d == 0 else slice(RG_ROWS - SUBLANES, RG_ROWS)
            at_start = sub == (0 if d == 0 else SUBLANES - 1)

            @pl.when(i == (0 if d == 0 else SEQ // RG_ROWS - 1))
            def _():
                p_s[d, pl.ds(r0 + edge.start, SUBLANES), :] = xn[edge] * jnp.where(at_start, 1.0, mult[edge])
        return carry

    lax.fori_loop(0, SEQ // RG_ROWS, gates, 0)

    def scan(j, carry):
        hf, pf, hb, pb = carry
        idx_f = grp(j)
        idx_b = grp(RG_SEG - 1 - j)
        af = a_s[0, idx_f, :]
        hf = af * hf + p_s[0, idx_f, :]
        pf = pf * af
        h_s[0, idx_f, :] = hf
        q_s[0, idx_f, :] = pf
        ab = a_s[1, idx_b, :]
        hb = ab * hb + p_s[1, idx_b, :]
        pb = pb * ab
        h_s[1, idx_b, :] = hb
        q_s[1, idx_b, :] = pb
        return hf, pf, hb, pb

    zero = jnp.zeros((SUBLANES, LANES), F32)
    one = jnp.ones((SUBLANES, LANES), F32)
    hf, pf, hb, pb = lax.fori_loop(0, RG_SEG, scan, (zero, one, zero, one), unroll=8)

    cf = zero
    for s in range(1, SUBLANES):
        cf = jnp.where(sub == s, pltpu.roll(hf + pf * cf, 1, axis=0), cf)
    cr = zero
    for s in range(SUBLANES - 2, -1, -1):
        cr = jnp.where(sub == s, pltpu.roll(hb + pb * cr, SUBLANES - 1, axis=0), cr)

    def finish(j, carry):
        rows = grp(j)
        h = h_s[0, rows, :] + q_s[0, rows, :] * cf + h_s[1, rows, :] + q_s[1, rows, :] * cr
        o_ref[0, pl.ds(j, SUBLANES, stride=RG_SEG), :] = h * jax.nn.gelu(gi_s[rows, :])
        return carry

    lax.fori_loop(0, RG_SEG, finish, 0, unroll=8)


def _rglru_gate_weights(gate_w, gate_b):
    nchunk = D_RG // LANES
    bd = D_RG // RG_BLOCKS
    per = LANES // bd
    blocks = gate_w.reshape(4, nchunk, per, bd, bd)
    blocks = jnp.transpose(blocks, (1, 2, 3, 0, 4))
    same_block = jnp.eye(per, dtype=F32)[None, :, None, None, :, None]
    w = blocks[:, :, :, :, None, :] * same_block
    w = w.reshape(nchunk, LANES, 4 * LANES).astype(BF16)
    b = gate_b.reshape(4, nchunk, LANES).transpose(1, 0, 2).reshape(nchunk, 1, 4 * LANES)
    return w, b


def _rglru_mixer(main, b, conv_w, conv_b, gate_w, gate_b, lam):
    nchunk = D_RG // LANES
    wg, bg = _rglru_gate_weights(gate_w, gate_b)
    main3 = main.reshape(b, SEQ, D_MAIN)
    blk = lambda off: pl.BlockSpec((1, SEQ, LANES), lambda i, c, off=off: (i, 0, off // LANES + c))
    yc = pl.pallas_call(
        _rglru_kernel,
        grid=(b, nchunk),
        in_specs=[blk(MAIN_RG), blk(MAIN_RG_GATE),
                  pl.BlockSpec((conv_w.shape[0], LANES), lambda i, c: (0, c)),
                  pl.BlockSpec((1, LANES), lambda i, c: (0, c)),
                  pl.BlockSpec((1, LANES, 4 * LANES), lambda i, c: (c, 0, 0)),
                  pl.BlockSpec((1, 1, 4 * LANES), lambda i, c: (c, 0, 0)),
                  pl.BlockSpec((2, LANES), lambda i, c: (0, c))],
        out_specs=pl.BlockSpec((1, SEQ, LANES), lambda i, c: (i, 0, c)),
        out_shape=jax.ShapeDtypeStruct((b, SEQ, D_RG), F32),
        scratch_shapes=[pltpu.VMEM((SEQ + RG_HALO * SUBLANES, LANES), F32), pltpu.VMEM((SEQ, LANES), F32)]
                       + [pltpu.VMEM((2, SEQ, LANES), F32)] * 4,
        compiler_params=_params("parallel", "parallel"),
        name="rglru",
    )(main3, main3, conv_w, conv_b.reshape(1, D_RG), wg, bg, lam)
    return yc.reshape(b * SEQ, D_RG)


MERGE_TM = 512


def _layer_norm(z, g, b):
    mu = jnp.mean(z, axis=-1, keepdims=True)
    zc = z - mu
    var = jnp.mean(zc * zc, axis=-1, keepdims=True)
    return zc * lax.rsqrt(var + LN_EPS) * g + b


def _time_major(ref, dil, scratch):
    if dil == 1:
        return ref[0, 0]
    tiles = D_GROUP // LANES
    for r in range(dil):
        for h in range(tiles):
            scratch[h, pl.ds(r, MERGE_TM // dil, stride=dil), :] = ref[0, r, :, h * LANES:(h + 1) * LANES]
    return jnp.concatenate([scratch[h] for h in range(tiles)], axis=1)


def _merge_kernel(x_ref, ya_ref, o0_ref, o1_ref, o2_ref, l0_ref, l1_ref, l2_ref, yc_ref,
                  wg_ref, bgate_ref, wa_ref, wb_ref, wc_ref, wo_ref, g_ref, b_ref, out_ref,
                  s0, s1, s2, s3):
    x = x_ref[...]
    xb = x.astype(BF16)
    d0, d1, d2 = DILATIONS
    o0, l0 = _time_major(o0_ref, d0, s0), _time_major(l0_ref, d0, s0)
    o1, l1 = _time_major(o1_ref, d1, s0), _time_major(l1_ref, d1, s1)
    o2, l2 = _time_major(o2_ref, d2, s2), _time_major(l2_ref, d2, s3)
    lmax = jnp.maximum(jnp.maximum(l0, l1), l2)
    e0, e1, e2 = jnp.exp(l0 - lmax), jnp.exp(l1 - lmax), jnp.exp(l2 - lmax)
    yb = (e0 * o0 + e1 * o1 + e2 * o2) / (e0 + e1 + e2)
    branches = ((ya_ref[...], wa_ref), (yb, wb_ref), (yc_ref[...], wc_ref))
    mixed = None
    for j, (y, w_ref) in enumerate(branches):
        cols = slice(j * D_MODEL, (j + 1) * D_MODEL)
        gate = jax.nn.sigmoid(_dot(xb, wg_ref[:, cols]) + bgate_ref[:, cols])
        term = gate * _dot(y.astype(BF16), w_ref[...])
        mixed = term if mixed is None else mixed + term
    z = DN_ALPHA * x + _dot(mixed.astype(BF16), wo_ref[...])
    out_ref[...] = _layer_norm(z, g_ref[...], b_ref[...])


def _merge_and_project(x2d, ya, att, yc, w_gate, b_gate, w_a, w_b, w_c, w_o, ln_g, ln_b):
    t = x2d.shape[0]
    tps = SEQ // MERGE_TM
    rows = lambda w: pl.BlockSpec((MERGE_TM, w), lambda i: (i, 0))
    grp = lambda d: pl.BlockSpec((1, d, MERGE_TM // d, D_GROUP), lambda i: (i // tps, 0, i % tps, 0))
    const = lambda a: pl.BlockSpec(a.shape, lambda i: (0,) * a.ndim, pipeline_mode=pl.Buffered(1))
    (o0, l0), (o1, l1), (o2, l2) = att
    weights = (w_gate, b_gate.reshape(1, -1), w_a, w_b, w_c, w_o, ln_g.reshape(1, -1), ln_b.reshape(1, -1))
    return pl.pallas_call(
        _merge_kernel,
        grid=(t // MERGE_TM,),
        in_specs=[rows(D_MODEL), rows(D_HY)] + [grp(d) for d in DILATIONS] * 2 + [rows(D_RG)]
                 + [const(a) for a in weights],
        out_specs=rows(D_MODEL),
        out_shape=jax.ShapeDtypeStruct((t, D_MODEL), F32),
        scratch_shapes=[pltpu.VMEM((D_GROUP // LANES, MERGE_TM, LANES), F32)] * 4,
        compiler_params=_params("parallel"),
        name="merge_project_norm",
    )(x2d, ya, o0, o1, o2, l0, l1, l2, yc, *weights)


FFN_TM = 512
FFN_CF = 1536


def _ffn_kernel(x_ref, prev_ref, next_ref, wup_ref, cw_ref, cb_ref, wdn_ref, g_ref, b_ref, out_ref):
    i = pl.program_id(0)
    tiles_per_seq = SEQ // FFN_TM
    x = x_ref[...]
    xb = x.astype(BF16)
    has_prev = (i % tiles_per_seq) != 0
    has_next = (i % tiles_per_seq) != tiles_per_seq - 1
    halo = jnp.concatenate([prev_ref[...], next_ref[...]], axis=0).astype(BF16)
    xcat = jnp.concatenate([xb, halo], axis=0)
    row = lax.broadcasted_iota(jnp.int32, (FFN_TM, FFN_CF), 0)
    acc = None
    for c in range(D_FF // FFN_CF):
        cols = slice(c * FFN_CF, (c + 1) * FFN_CF)
        ucols = slice(D_FF + c * FFN_CF, D_FF + (c + 1) * FFN_CF)
        hgx = _dot(xcat, wup_ref[:, cols])
        hg = hgx[:FFN_TM]
        before = jnp.where(has_prev, hgx[FFN_TM + SUBLANES - 1:FFN_TM + SUBLANES, :], 0.0)
        after = jnp.where(has_next, hgx[FFN_TM + SUBLANES:FFN_TM + SUBLANES + 1, :], 0.0)
        up = jnp.where(row == 0, before, pltpu.roll(hg, 1, axis=0))
        dn = jnp.where(row == FFN_TM - 1, after, pltpu.roll(hg, FFN_TM - 1, axis=0))
        w = cw_ref[:, cols]
        conv = up * w[0:1, :] + hg * w[1:2, :] + dn * w[2:3, :] + cb_ref[:, cols]
        act = jax.nn.gelu(conv) * _dot(xb, wup_ref[:, ucols])
        term = _dot(act.astype(BF16), wdn_ref[cols, :])
        acc = term if acc is None else acc + term
    out_ref[...] = _layer_norm(DN_ALPHA * x + acc, g_ref[...], b_ref[...])


def _ffn(x2d, w_up, conv_w, conv_b, w_down, ln_g, ln_b):
    t = x2d.shape[0]
    bpt = FFN_TM // SUBLANES
    nblk = t // SUBLANES
    const = lambda a: pl.BlockSpec(a.shape, lambda i: (0,) * a.ndim, pipeline_mode=pl.Buffered(1))
    weights = (w_up, conv_w, conv_b.reshape(1, -1), w_down, ln_g.reshape(1, -1), ln_b.reshape(1, -1))
    return pl.pallas_call(
        _ffn_kernel,
        grid=(t // FFN_TM,),
        in_specs=[pl.BlockSpec((FFN_TM, D_MODEL), lambda i: (i, 0)),
                  pl.BlockSpec((SUBLANES, D_MODEL), lambda i: (jnp.maximum(i * bpt - 1, 0), 0)),
                  pl.BlockSpec((SUBLANES, D_MODEL), lambda i: (jnp.minimum((i + 1) * bpt, nblk - 1), 0))]
                 + [const(a) for a in weights],
        out_specs=pl.BlockSpec((FFN_TM, D_MODEL), lambda i: (i, 0)),
        out_shape=jax.ShapeDtypeStruct((t, D_MODEL), F32),
        compiler_params=_params("parallel"),
        name="ffn_norm",
    )(x2d, x2d, x2d, *weights)


def _encoder_layer(x2d, b, l, p, khat):
    main, *att = _input_projection(x2d, p["w_in"][l])
    ya = _hyena_mixer(main, b, p["hy_conv_w"][l], p["hy_conv_b"][l], khat[l], p["hy_bias"][l])
    att_out = [_dilated_attention_group(a, dil) for a, dil in zip(att, DILATIONS)]
    yc = _rglru_mixer(main, b, p["rg_conv_w"][l], p["rg_conv_b"][l], p["rg_gate_w"][l], p["rg_gate_b"][l],
                      p["rg_lam"][l])
    x2d = _merge_and_project(x2d, ya, att_out, yc,
                             p["w_gate"][l], p["b_gate"][l], p["w_br_a"][l], p["w_br_b"][l], p["w_br_c"][l],
                             p["w_o"][l], p["ln1_g"][l], p["ln1_b"][l])
    return _ffn(x2d, p["w_up"][l], p["ffn_conv_w"][l], p["ffn_conv_b"][l], p["w_down"][l],
                p["ln2_g"][l], p["ln2_b"][l])


def _trunk(x, p):
    b = x.shape[0]
    khat = _hyena_filter_spectra(p["hy_filt_w1"], p["hy_filt_b1"], p["hy_filt_w2"], p["hy_filt_b2"],
                                 p["hy_filt_w3"], p["hy_filt_b3"], p["hy_filt_freq"])
    x2d = x.reshape(b * SEQ, D_MODEL)
    for l in range(DEPTH):
        x2d = _encoder_layer(x2d, b, l, p, khat)
    return x2d.reshape(b, SEQ, D_MODEL)


_MATMUL_WEIGHTS = ("w_in", "w_gate", "w_br_a", "w_br_b", "w_br_c", "w_o", "w_up", "w_down")


def kernel(x_prompt, x_sample, w_in, hy_conv_w, hy_conv_b, hy_filt_w1, hy_filt_b1, hy_filt_w2, hy_filt_b2, hy_filt_w3, hy_filt_b3, hy_filt_freq, hy_bias, rg_conv_w, rg_conv_b, rg_gate_w, rg_gate_b, rg_lam, w_gate, b_gate, w_br_a, w_br_b, w_br_c, w_o, ln1_g, ln1_b, w_up, ffn_conv_w, ffn_conv_b, w_down, ln2_g, ln2_b):
    p = dict(w_in=w_in, hy_conv_w=hy_conv_w, hy_conv_b=hy_conv_b, hy_filt_w1=hy_filt_w1,
             hy_filt_b1=hy_filt_b1, hy_filt_w2=hy_filt_w2, hy_filt_b2=hy_filt_b2, hy_filt_w3=hy_filt_w3,
             hy_filt_b3=hy_filt_b3, hy_filt_freq=hy_filt_freq, hy_bias=hy_bias, rg_conv_w=rg_conv_w,
             rg_conv_b=rg_conv_b, rg_gate_w=rg_gate_w, rg_gate_b=rg_gate_b, rg_lam=rg_lam, w_gate=w_gate,
             b_gate=b_gate, w_br_a=w_br_a, w_br_b=w_br_b, w_br_c=w_br_c, w_o=w_o, ln1_g=ln1_g, ln1_b=ln1_b,
             w_up=w_up, ffn_conv_w=ffn_conv_w, ffn_conv_b=ffn_conv_b, w_down=w_down, ln2_g=ln2_g,
             ln2_b=ln2_b)
    for name in _MATMUL_WEIGHTS:
        p[name] = p[name].astype(BF16)
    nb = x_prompt.shape[0]
    y = _trunk(jnp.concatenate([x_prompt, x_sample], axis=0), p)
    return (y[:nb], y[nb:])
```

```python
import functools
import math

import numpy as np
import jax
import jax.numpy as jnp
from jax import lax
from jax.experimental import pallas as pl
from jax.experimental.pallas import tpu as pltpu

F32 = jnp.float32
BF16 = jnp.bfloat16

D_MODEL = 1024
SEQ = 4096
DEPTH = 2
D_HY = 512
HY_EMB_BANDS = 8
HY_EMB_PAD = 32
HY_FAST_DECAY = 0.3
HY_SLOW_DECAY = 1.5
HY_TARGET = 1e-2
HEAD_DIM = 64
HEADS_PER_GROUP = 4
DILATIONS = (1, 4, 16)
RADIUS = 64
D_GROUP = HEADS_PER_GROUP * HEAD_DIM
D_ATT = len(DILATIONS) * D_GROUP
ROPE_THETA = 10000.0
NEG_INF = -1e30
DEN_FLOOR = 1e-30
D_RG = 512
RG_BLOCKS = 8
RG_C = 8.0
D_FF = 3 * D_MODEL
DN_ALPHA = (2 * DEPTH) ** 0.25
LN_EPS = 1e-5
D_IN = 3 * D_HY + 3 * D_ATT + 2 * D_RG
OFF_Q = 3 * D_HY
OFF_K = OFF_Q + D_ATT
OFF_V = OFF_K + D_ATT
OFF_RG = OFF_V + D_ATT
D_MAIN = 3 * D_HY + 2 * D_RG
MAIN_RG = 3 * D_HY
MAIN_RG_GATE = MAIN_RG + D_RG

LANES = 128
SUBLANES = 8
VMEM_LIMIT = 56 * 1024 * 1024

FFT_N = 2 * SEQ
FFT_N1 = 64
FFT_N2 = 128
FFT_HALF = FFT_N1 // 2
HY_COLS = FFT_N2 * D_HY
HY_SLABS = FFT_N1 // 2 + 1
HY_SLABS_PAD = 40


def _params(*sem):
    return pltpu.CompilerParams(dimension_semantics=sem, vmem_limit_bytes=VMEM_LIMIT)


def _dot(a, b):
    return jnp.dot(a, b, preferred_element_type=F32)


def _cos_sin(a, b, period):
    ang = 2.0 * np.pi * (np.outer(a, b) % period) / period
    return np.cos(ang), np.sin(ang)


@functools.lru_cache(maxsize=None)
def _dft_constants():
    n2 = np.arange(FFT_N2)
    k2 = np.arange(FFT_N2)
    g = np.zeros((HY_SLABS_PAD, 2 * FFT_N2, 2 * FFT_N2), np.float64)
    for a in range(HY_SLABS_PAD):
        c, s = _cos_sin(a + FFT_N1 * k2, n2, FFT_N)
        gr, gi = c, -s
        g[a] = np.block([[gr, -gi], [gi, gr]])
    c, s = _cos_sin(np.arange(HY_SLABS_PAD), np.arange(FFT_N1), FFT_N1)
    m1_full = np.kron(np.concatenate([c, -s], axis=0), np.eye(SUBLANES))
    return jnp.asarray(m1_full, BF16), jnp.asarray(g, BF16)


@functools.lru_cache(maxsize=None)
def _hyena_stage_constants():
    k1 = np.arange(HY_SLABS_PAD)
    n1 = np.arange(FFT_HALF)
    eye = np.eye(SUBLANES)
    c, s = _cos_sin(k1, n1, FFT_N1)
    m1 = np.kron(np.concatenate([c, -s], axis=0), eye)
    wgt = np.where((k1 == 0) | (k1 == FFT_N1 // 2), 1.0, 2.0) * (k1 < HY_SLABS)
    c, s = _cos_sin(n1, k1, FFT_N1)
    m3 = np.kron(np.concatenate([c * wgt, -s * wgt], axis=1) / FFT_N, eye)
    return jnp.asarray(m1, BF16), jnp.asarray(m3, BF16)


@functools.lru_cache(maxsize=None)
def _filter_positions():
    L = SEQ
    t = np.linspace(0.0, 1.0, L, dtype=np.float32).astype(np.float64)[:, None]
    w = (2.0 * np.pi * np.arange(L, dtype=np.float32) / L).astype(np.float64)[:, None]
    bands = np.linspace(1e-4, HY_EMB_BANDS - 1, HY_EMB_BANDS, dtype=np.float32).astype(np.float64)[None, :]
    z = np.concatenate([t, np.cos(bands * w), -np.sin(bands * w)], axis=-1)
    m = np.arange(2 * L)
    src = np.where(m < L, m, 2 * L - m)
    src = np.where(m == L, 0, src)
    zp = np.zeros((2 * L, HY_EMB_PAD))
    zp[:, :z.shape[1]] = z[src]
    return jnp.asarray(zp, F32)


@functools.lru_cache(maxsize=None)
def _rope_tables():
    inv = ROPE_THETA ** (-np.arange(0, HEAD_DIM, 2, dtype=np.float32).astype(np.float64) / HEAD_DIM)
    ang = np.arange(SEQ, dtype=np.float64)[:, None] * inv[None, :]
    cos = np.concatenate([np.cos(ang), np.cos(ang)], axis=1)
    sin = np.concatenate([-np.sin(ang), np.sin(ang)], axis=1)
    cos = np.tile(cos, (1, HEADS_PER_GROUP))
    sin = np.tile(sin, (1, HEADS_PER_GROUP))
    return jnp.asarray(cos, F32), jnp.asarray(sin, F32)


PROJ_TM = 512
_MAIN_CHUNKS = tuple((src, dst, 512) for src, dst in
                     [(j, j) for j in range(0, 3 * D_HY, 512)] +
                     [(OFF_RG + j, MAIN_RG + j) for j in range(0, 2 * D_RG, 512)])


def _rope(t, cos, sin):
    lane = lax.broadcasted_iota(jnp.int32, t.shape, 1)
    first_half = (lane % HEAD_DIM) < (HEAD_DIM // 2)
    rot = jnp.where(first_half, pltpu.roll(t, D_GROUP - HEAD_DIM // 2, axis=1),
                    pltpu.roll(t, HEAD_DIM // 2, axis=1))
    return t * cos + rot * sin


def _proj_kernel(*refs, first_tiles):
    x_refs = refs[:len(refs) - 8]
    w_ref, cos_ref, sin_ref, main_ref, a0_ref, a1_ref, a2_ref, qkv_s = refs[len(x_refs):]
    if len(x_refs) == 1:
        x = x_refs[0][...]
    else:
        x = jnp.where(pl.program_id(0) < first_tiles, x_refs[0][...], x_refs[1][...])
    xb = x.astype(BF16)
    for src, dst, width in _MAIN_CHUNKS:
        main_ref[:, dst:dst + width] = _dot(xb, w_ref[:, src:src + width])
    cos = cos_ref[...]
    sin = sin_ref[...]
    for g, (dil, att_ref) in enumerate(zip(DILATIONS, (a0_ref, a1_ref, a2_ref))):
        col = lambda off: slice(off + g * D_GROUP, off + (g + 1) * D_GROUP)
        q = _rope(_dot(xb, w_ref[:, col(OFF_Q)]), cos, sin) * (HEAD_DIM ** -0.5)
        k = _rope(_dot(xb, w_ref[:, col(OFF_K)]), cos, sin)
        v = _dot(xb, w_ref[:, col(OFF_V)])
        if dil == 1:
            att_ref[0, 0, :, 0:D_GROUP] = q
            att_ref[0, 0, :, D_GROUP:2 * D_GROUP] = k
            att_ref[0, 0, :, 2 * D_GROUP:] = v
        else:
            for j, t in enumerate((q, k, v)):
                for h in range(D_GROUP // LANES):
                    qkv_s[j * (D_GROUP // LANES) + h] = t[:, h * LANES:(h + 1) * LANES]
            for r in range(dil):
                for j in range(3 * D_GROUP // LANES):
                    att_ref[0, r, :, j * LANES:(j + 1) * LANES] = qkv_s[j, pl.ds(r, PROJ_TM // dil, stride=dil), :]


def _input_projection(xs, w_in_bf16):
    tiles = [x.shape[0] // PROJ_TM for x in xs]
    t = sum(tiles) * PROJ_TM
    b = t // SEQ
    tps = SEQ // PROJ_TM
    cos, sin = _rope_tables()
    tab = pl.BlockSpec((PROJ_TM, D_GROUP), lambda i: (i % tps, 0))
    att_spec = lambda d: pl.BlockSpec((1, d, PROJ_TM // d, 3 * D_GROUP), lambda i: (i // tps, 0, i % tps, 0))
    att_shape = lambda d: jax.ShapeDtypeStruct((b, d, SEQ // d, 3 * D_GROUP), F32)
    x_specs = [pl.BlockSpec((PROJ_TM, D_MODEL),
                            lambda i, off=sum(tiles[:k]), last=tiles[k] - 1: (jnp.clip(i - off, 0, last), 0))
               for k in range(len(xs))]
    return pl.pallas_call(
        functools.partial(_proj_kernel, first_tiles=tiles[0]),
        grid=(t // PROJ_TM,),
        in_specs=x_specs + [pl.BlockSpec((D_MODEL, D_IN), lambda i: (0, 0), pipeline_mode=pl.Buffered(1)),
                            tab, tab],
        out_specs=[pl.BlockSpec((PROJ_TM, D_MAIN), lambda i: (i, 0))] + [att_spec(d) for d in DILATIONS],
        out_shape=[jax.ShapeDtypeStruct((t, D_MAIN), F32)] + [att_shape(d) for d in DILATIONS],
        scratch_shapes=[pltpu.VMEM((3 * D_GROUP // LANES, PROJ_TM, LANES), F32)],
        compiler_params=_params("parallel"),
        name="input_projection",
    )(*xs, w_in_bf16, cos, sin)


def _conv_rows(x, w, bias, pad_left):
    n = x.shape[0]
    row = lax.broadcasted_iota(jnp.int32, x.shape, 0)
    acc = None
    for k in range(w.shape[0]):
        off = k - pad_left
        if off == 0:
            xs = x
        else:
            xs = pltpu.roll(x, (-off) % n, axis=0)
            valid = (row >= -off) if off < 0 else (row < n - off)
            xs = jnp.where(valid, xs, 0.0)
        term = xs * w[k:k + 1, :]
        acc = term if acc is None else acc + term
    return acc + bias


FILT_ROWS = 512


def _hy_filter_kernel(z_ref, w1_ref, b1_ref, w2_ref, b2_ref, w3f_ref, w3b_ref, b3f_ref, b3b_ref,
                      fr_ref, dl_ref, m1_ref, g_ref, kh_ref, h_s, k_s, a_s):
    hi = lax.Precision.HIGHEST
    w3f, w3b, b3f, b3b = w3f_ref[0], w3b_ref[0], b3f_ref[0], b3b_ref[0]
    delta = dl_ref[...]

    @pl.when(pl.program_id(1) == 0)
    def _():
        fr = fr_ref[0]
        w1, b1, w2, b2 = w1_ref[0], b1_ref[0], w2_ref[0], b2_ref[0]

        def hidden(i, carry):
            rows = pl.ds(pl.multiple_of(i * FILT_ROWS, FILT_ROWS), FILT_ROWS)
            h = jnp.sin(fr * (jnp.dot(z_ref[rows, :], w1, precision=hi, preferred_element_type=F32) + b1))
            h_s[rows, :] = jnp.sin(fr * (jnp.dot(h, w2, precision=hi, preferred_element_type=F32) + b2))
            return carry

        lax.fori_loop(0, FFT_N // FILT_ROWS, hidden, 0)

    def body(i, total):
        r0 = pl.multiple_of(i * FILT_ROWS, FILT_ROWS)
        z = z_ref[pl.ds(r0, FILT_ROWS), :]
        h = h_s[pl.ds(r0, FILT_ROWS), :]
        hf = jnp.dot(h, w3f, precision=hi, preferred_element_type=F32) + b3f
        hb = jnp.dot(h, w3b, precision=hi, preferred_element_type=F32) + b3b
        m = lax.broadcasted_iota(jnp.int32, hf.shape, 0) + r0
        decay = jnp.exp(-z[:, 0:1] * delta)
        val = jnp.where(m < SEQ, hf, hb) * decay
        val = jnp.where(m == SEQ, 0.0, val)
        k_s[pl.ds(r0, FILT_ROWS), :] = val
        return total + jnp.sum(jnp.abs(val), axis=0, keepdims=True)

    total = lax.fori_loop(0, FFT_N // FILT_ROWS, body, jnp.zeros((1, LANES), F32))

    def scale(i, carry):
        r0 = pl.multiple_of(i * FILT_ROWS, FILT_ROWS)
        k_s[pl.ds(r0, FILT_ROWS), :] = k_s[pl.ds(r0, FILT_ROWS), :] / total
        return carry

    lax.fori_loop(0, FFT_N // FILT_ROWS, scale, 0)

    def stage1(j, carry):
        n2 = pl.multiple_of(j * SUBLANES, SUBLANES)
        x = jnp.concatenate([k_s[pl.ds(n1 * FFT_N2 + n2, SUBLANES), :] for n1 in range(FFT_N1)], axis=0)
        a = _dot(m1_ref[...], x.astype(BF16)).reshape(2 * HY_SLABS_PAD, SUBLANES, LANES)
        a_s[0, :, pl.ds(n2, SUBLANES), :] = a[:HY_SLABS_PAD]
        a_s[1, :, pl.ds(n2, SUBLANES), :] = a[HY_SLABS_PAD:]
        return carry

    lax.fori_loop(0, FFT_N2 // SUBLANES, stage1, 0)

    def stage2(k1, carry):
        a = jnp.concatenate([a_s[0, k1], a_s[1, k1]], axis=0).astype(BF16)
        x = _dot(g_ref[k1], a)
        rows = pl.ds(pl.multiple_of(k1 * FFT_N2, FFT_N2), FFT_N2)
        kh_ref[0, 0, rows, :] = x[:FFT_N2]
        kh_ref[0, 1, rows, :] = x[FFT_N2:]
        return carry

    lax.fori_loop(0, HY_SLABS, stage2, 0)


def _hyena_filter_spectra(w1, b1, w2, b2, w3, b3, freq):
    nchunk = D_HY // LANES
    m1_full, g = _dft_constants()
    z = _filter_positions()
    emb = z.shape[1]
    w1 = jnp.pad(w1, ((0, 0), (0, emb - w1.shape[1]), (0, 0)))
    hid = w1.shape[-1]
    deltas = np.abs(np.linspace(math.log(HY_TARGET) / HY_SLOW_DECAY, math.log(HY_TARGET) / HY_FAST_DECAY,
                                D_HY, dtype=np.float32))
    deltas = jnp.asarray(deltas, F32).reshape(1, D_HY)
    row = lambda a: a.reshape(DEPTH, 1, a.shape[-1])
    full = lambda s: pl.BlockSpec((1,) + s, lambda l, c: (l, 0, 0))
    return pl.pallas_call(
        _hy_filter_kernel,
        grid=(DEPTH, nchunk),
        in_specs=[pl.BlockSpec((FFT_N, emb), lambda l, c: (0, 0)),
                  full((emb, hid)), full((1, hid)), full((hid, hid)), full((1, hid)),
                  pl.BlockSpec((1, hid, LANES), lambda l, c: (l, 0, c)),
                  pl.BlockSpec((1, hid, LANES), lambda l, c: (l, 0, nchunk + c)),
                  pl.BlockSpec((1, 1, LANES), lambda l, c: (l, 0, c)),
                  pl.BlockSpec((1, 1, LANES), lambda l, c: (l, 0, nchunk + c)),
                  full((1, hid)),
                  pl.BlockSpec((1, LANES), lambda l, c: (0, c)),
                  pl.BlockSpec(m1_full.shape, lambda l, c: (0, 0), pipeline_mode=pl.Buffered(1)),
                  pl.BlockSpec(g.shape, lambda l, c: (0, 0, 0), pipeline_mode=pl.Buffered(1))],
        out_specs=pl.BlockSpec((1, 2, HY_SLABS * FFT_N2, LANES), lambda l, c: (l, 0, 0, c)),
        out_shape=jax.ShapeDtypeStruct((DEPTH, 2, HY_SLABS * FFT_N2, D_HY), F32),
        scratch_shapes=[pltpu.VMEM((FFT_N, hid), F32), pltpu.VMEM((FFT_N, LANES), F32),
                        pltpu.VMEM((2, HY_SLABS_PAD, FFT_N2, LANES), F32)],
        compiler_params=_params("parallel", "arbitrary"),
        name="hyena_filter",
    )(z, w1, row(b1), w2, row(b2), w3, w3, row(b3), row(b3), row(freq), deltas, m1_full, g)


HY_SLAB_UNROLL = 11
HY_EDGE_UNROLL = 4


def _hyena_kernel(x0_ref, x1_ref, v_ref, cw_ref, cb_ref, m1_ref, m3_ref, g_ref, kh_ref, bias_ref, o_ref,
                  uu_s, x0_s, a_s):
    shape3 = (FFT_HALF, FFT_N2, LANES)
    nat = lambda ref: ref[0].reshape(SEQ, LANES)
    x0_s[...] = _conv_rows(nat(x0_ref), cw_ref[:, 0, :], cb_ref[0], 1).reshape(shape3)
    x1 = _conv_rows(nat(x1_ref), cw_ref[:, 1, :], cb_ref[1], 1)
    v = _conv_rows(nat(v_ref), cw_ref[:, 2, :], cb_ref[2], 1)
    uu_s[...] = (v * x1).reshape(shape3)

    def stage1(j, carry):
        blocks = [pl.ds(pl.multiple_of((j * HY_EDGE_UNROLL + u) * SUBLANES, SUBLANES), SUBLANES)
                  for u in range(HY_EDGE_UNROLL)]
        xs = [uu_s[:, n2, :].reshape(FFT_HALF * SUBLANES, LANES).astype(BF16) for n2 in blocks]
        res = [_dot(m1_ref[...], xb).reshape(2 * HY_SLABS_PAD, SUBLANES, LANES) for xb in xs]
        for n2, a in zip(blocks, res):
            a_s[0, :, n2, :] = a[:HY_SLABS_PAD]
            a_s[1, :, n2, :] = a[HY_SLABS_PAD:]
        return carry

    lax.fori_loop(0, FFT_N2 // SUBLANES // HY_EDGE_UNROLL, stage1, 0)

    def stage2(j, carry):
        slabs = [j * HY_SLAB_UNROLL + u for u in range(HY_SLAB_UNROLL)]
        gs = [g_ref[k1] for k1 in slabs]
        xs = [jnp.concatenate([a_s[0, k1], a_s[1, k1]], axis=0).astype(BF16) for k1 in slabs]
        res = []
        for k1, g, a in zip(slabs, gs, xs):
            x = _dot(g, a)
            xr, xi = x[:FFT_N2], x[FFT_N2:]
            rows = pl.ds(pl.multiple_of(k1 * FFT_N2, FFT_N2), FFT_N2)
            kr, ki = kh_ref[0, rows, :], kh_ref[1, rows, :]
            z = jnp.concatenate([xr * kr - xi * ki, xr * ki + xi * kr], axis=0).astype(BF16)
            res.append(lax.dot_general(g, z, (((0,), (0,)), ((), ())), preferred_element_type=F32))
        for k1, d in zip(slabs, res):
            a_s[0, k1] = d[:FFT_N2]
            a_s[1, k1] = d[FFT_N2:]
        return carry

    lax.fori_loop(0, HY_SLABS // HY_SLAB_UNROLL, stage2, 0)

    bias = bias_ref[...]

    def stage3(j, carry):
        blocks = [pl.ds(pl.multiple_of((j * HY_EDGE_UNROLL + u) * SUBLANES, SUBLANES), SUBLANES)
                  for u in range(HY_EDGE_UNROLL)]
        ds = [jnp.concatenate([a_s[0, :, n2, :], a_s[1, :, n2, :]], axis=0)
              .reshape(2 * HY_SLABS_PAD * SUBLANES, LANES).astype(BF16) for n2 in blocks]
        ys = [_dot(m3_ref[...], d).reshape(FFT_HALF, SUBLANES, LANES) for d in ds]
        for n2, y in zip(blocks, ys):
            o_ref[0, :, n2, :] = x0_s[:, n2, :] * (y + uu_s[:, n2, :] * bias)
        return carry

    lax.fori_loop(0, FFT_N2 // SUBLANES // HY_EDGE_UNROLL, stage3, 0)


def _hyena_mixer(main, b, conv_w, conv_b, khat, hy_bias):
    nchunk = D_HY // LANES
    m1, m3 = _hyena_stage_constants()
    _, g = _dft_constants()
    main4 = main.reshape(b, FFT_HALF, FFT_N2, D_MAIN)
    blk = lambda part: pl.BlockSpec((1, FFT_HALF, FFT_N2, LANES),
                                    lambda c, i, part=part: (i, 0, 0, part * nchunk + c))
    const = lambda a: pl.BlockSpec(a.shape, lambda c, i: (0,) * a.ndim, pipeline_mode=pl.Buffered(1))
    rows = HY_SLABS * FFT_N2
    ya = pl.pallas_call(
        _hyena_kernel,
        grid=(nchunk, b),
        in_specs=[blk(0), blk(1), blk(2),
                  pl.BlockSpec((3, 3, LANES), lambda c, i: (0, 0, c)),
                  pl.BlockSpec((3, 1, LANES), lambda c, i: (0, 0, c)),
                  const(m1), const(m3),
                  pl.BlockSpec((HY_SLABS_PAD, 2 * FFT_N2, 2 * FFT_N2), lambda c, i: (0, 0, 0),
                               pipeline_mode=pl.Buffered(1)),
                  pl.BlockSpec((2, rows, LANES), lambda c, i: (0, 0, c), pipeline_mode=pl.Buffered(1)),
                  pl.BlockSpec((1, LANES), lambda c, i: (0, c))],
        out_specs=pl.BlockSpec((1, FFT_HALF, FFT_N2, LANES), lambda c, i: (i, 0, 0, c)),
        out_shape=jax.ShapeDtypeStruct((b, FFT_HALF, FFT_N2, D_HY), F32),
        scratch_shapes=[pltpu.VMEM((FFT_HALF, FFT_N2, LANES), F32),
                        pltpu.VMEM((FFT_HALF, FFT_N2, LANES), F32),
                        pltpu.VMEM((2, HY_SLABS_PAD, FFT_N2, LANES), F32)],
        compiler_params=_params("parallel", "parallel"),
        name="hyena",
    )(main4, main4, main4, conv_w.reshape(3, 3, D_HY), conv_b.reshape(3, 1, D_HY), m1, m3, g, khat,
      hy_bias.reshape(1, D_HY))
    return ya.reshape(b * SEQ, D_HY)


ATT_TQ = 128
ATT_WIN = ATT_TQ + 2 * RADIUS
ATT_QC = 1024
ATT_UNROLL = 4
ATT_MAX_RESIDUE_STEPS = 4


def _attn_kernel(q_ref, kp_ref, k_ref, kn_ref, vp_ref, v_ref, vn_ref, o_ref, lse_ref, qs, ks, vs, *, n, qc, rpb):
    c0 = pl.program_id(2) * qc
    lane = lax.broadcasted_iota(jnp.int32, (qc, D_GROUP), 1)
    even_head = (lane % LANES) < HEAD_DIM
    lane_p = lax.broadcasted_iota(jnp.int32, (ATT_TQ, LANES), 1)
    low_head = lane_p < HEAD_DIM
    row = lax.broadcasted_iota(jnp.int32, (ATT_TQ, ATT_WIN), 0)
    col = lax.broadcasted_iota(jnp.int32, (ATT_TQ, ATT_WIN), 1)
    band = jnp.abs(col - RADIUS - row) - RADIUS
    unroll = min(ATT_UNROLL, qc // ATT_TQ)

    def residue(rr, carry):
        q = q_ref[0, rr]
        qs[0] = jnp.where(even_head, q, 0.0).astype(BF16)
        qs[1] = jnp.where(even_head, 0.0, q).astype(BF16)
        ks[0:RADIUS] = kp_ref[0, rr].astype(BF16)
        ks[RADIUS:RADIUS + qc] = k_ref[0, rr].astype(BF16)
        ks[RADIUS + qc:] = kn_ref[0, rr].astype(BF16)
        vs[0:RADIUS] = vp_ref[0, rr].astype(BF16)
        vs[RADIUS:RADIUS + qc] = v_ref[0, rr].astype(BF16)
        vs[RADIUS + qc:] = vn_ref[0, rr].astype(BF16)

        def query_block(q0):
            pos = col + (c0 + q0 - RADIUS)
            mask = jnp.maximum(jnp.maximum(band, -pos), pos - (n - 1)) <= 0
            for hp in range(D_GROUP // LANES):
                ls = slice(hp * LANES, (hp + 1) * LANES)
                kp = ks[pl.ds(q0, ATT_WIN), ls]
                vp = vs[pl.ds(q0, ATT_WIN), ls]
                outs, lses = [], []
                for h in range(2):
                    s = lax.dot_general(qs[h, pl.ds(q0, ATT_TQ), ls], kp, (((1,), (1,)), ((), ())),
                                        preferred_element_type=F32)
                    s = jnp.where(mask, s, NEG_INF)
                    m = jnp.max(s, axis=-1, keepdims=True)
                    e = jnp.where(mask, jnp.exp(s - m), 0.0)
                    den = jnp.maximum(jnp.sum(e, axis=-1, keepdims=True), DEN_FLOOR)
                    outs.append(_dot(e.astype(BF16), vp) / den)
                    lses.append(m + jnp.log(den))
                o_ref[0, rr, pl.ds(q0, ATT_TQ), ls] = jnp.where(low_head, outs[0], outs[1])
                lse_ref[0, rr, pl.ds(q0, ATT_TQ), ls] = jnp.where(low_head, lses[0], lses[1])

        def body(i, inner):
            for u in range(unroll):
                query_block(pl.multiple_of((i * unroll + u) * ATT_TQ, ATT_TQ))
            return inner

        lax.fori_loop(0, qc // ATT_TQ // unroll, body, 0)
        return carry

    lax.fori_loop(0, rpb, residue, 0)


def _dilated_attention_group(att, dil):
    b = att.shape[0]
    n = SEQ // dil
    qc = min(n, ATT_QC)
    hpc = qc // RADIUS
    last = n // RADIUS - 1
    rpb = max(1, dil // ATT_MAX_RESIDUE_STEPS)

    def halo_specs(col):
        return [pl.BlockSpec((1, rpb, RADIUS, D_GROUP), lambda i, r, c: (i, r, jnp.maximum(c * hpc - 1, 0), col)),
                pl.BlockSpec((1, rpb, qc, D_GROUP), lambda i, r, c: (i, r, c, col)),
                pl.BlockSpec((1, rpb, RADIUS, D_GROUP), lambda i, r, c: (i, r, jnp.minimum((c + 1) * hpc, last), col))]

    out = pl.BlockSpec((1, rpb, qc, D_GROUP), lambda i, r, c: (i, r, c, 0))
    return pl.pallas_call(
        functools.partial(_attn_kernel, n=n, qc=qc, rpb=rpb),
        grid=(b, dil // rpb, n // qc),
        in_specs=[halo_specs(0)[1]] + halo_specs(1) + halo_specs(2),
        out_specs=[out, out],
        out_shape=[jax.ShapeDtypeStruct((b, dil, n, D_GROUP), F32)] * 2,
        scratch_shapes=[pltpu.VMEM((2, qc, D_GROUP), BF16), pltpu.VMEM((qc + 2 * RADIUS, D_GROUP), BF16),
                        pltpu.VMEM((qc + 2 * RADIUS, D_GROUP), BF16)],
        compiler_params=_params("parallel", "parallel", "parallel"),
        name=f"dilated_attention_d{dil}",
    )(att, att, att, att, att, att, att)


RG_ROWS = 512
RG_SEG = SEQ // SUBLANES
RG_HALO = 3


def _rglru_kernel(xr_ref, gate_ref, cw_ref, cb_ref, wg_ref, bg_ref, lam_ref, o_ref,
                  xe_s, gi_s, a_s, p_s, h_s, q_s):
    grp = lambda j: pl.ds(pl.multiple_of(j * SUBLANES, SUBLANES), SUBLANES)

    def gather(j, carry):
        src = pl.ds(j, SUBLANES, stride=RG_SEG)
        xe_s[grp(j + 2), :] = xr_ref[0, src, :]
        gi_s[grp(j), :] = gate_ref[0, src, :]
        return carry

    lax.fori_loop(0, RG_SEG, gather, 0, unroll=8)

    sub = lax.broadcasted_iota(jnp.int32, (SUBLANES, LANES), 0)
    vreg = lambda j: slice((j + 2) * SUBLANES, (j + 3) * SUBLANES)
    for j in (-2, -1):
        xe_s[vreg(j), :] = jnp.where(sub == 0, 0.0, pltpu.roll(xe_s[vreg(RG_SEG + j), :], 1, axis=0))
    xe_s[vreg(RG_SEG), :] = jnp.where(sub == SUBLANES - 1, 0.0,
                                      pltpu.roll(xe_s[vreg(0), :], SUBLANES - 1, axis=0))

    lam = lam_ref[...]
    softplus = jnp.maximum(-lam, 0.0) + jnp.log1p(jnp.exp(-jnp.abs(lam)))
    log_a_scale = -RG_C * softplus
    wg = wg_ref[0]
    bg = bg_ref[0]
    cw = cw_ref[...]
    cb = cb_ref[...]

    def gates(i, carry):
        r0 = pl.multiple_of(i * RG_ROWS, RG_ROWS)
        xr = cb
        for k in range(RG_HALO + 1):
            xr = xr + cw[k:k + 1, :] * xe_s[pl.ds(r0 + k * SUBLANES, RG_ROWS), :]
        g = _dot(xr.astype(BF16), wg) + bg
        for d in range(2):
            r = 0.5 + 0.5 * jnp.tanh(0.5 * g[:, (2 * d) * LANES:(2 * d + 1) * LANES])
            ig = 0.5 + 0.5 * jnp.tanh(0.5 * g[:, (2 * d + 1) * LANES:(2 * d + 2) * LANES])
            log_a = r * log_a_scale[d:d + 1, :]
            a = jnp.exp(log_a)
            t = jnp.tanh(log_a)
            mult = jnp.sqrt(-2.0 * t / (1.0 - t))
            xn = xr * ig
            a_s[d, pl.ds(r0, RG_ROWS), :] = a
            p_s[d, pl.ds(r0, RG_ROWS), :] = xn * mult
            edge = slice(0, SUBLANES) if d == 0 else slice(RG_ROWS - SUBLANES, RG_ROWS)
            at_start = sub == (0 if d == 0 else SUBLANES - 1)

            @pl.when(i == (0 if d == 0 else SEQ // RG_ROWS - 1))
            def _():
                p_s[d, pl.ds(r0 + edge.start, SUBLANES), :] = xn[edge] * jnp.where(at_start, 1.0, mult[edge])
        return carry

    lax.fori_loop(0, SEQ // RG_ROWS, gates, 0)

    def scan(j, carry):
        hf, pf, hb, pb = carry
        idx_f = grp(j)
        idx_b = grp(RG_SEG - 1 - j)
        af = a_s[0, idx_f, :]
        hf = af * hf + p_s[0, idx_f, :]
        pf = pf * af
        h_s[0, idx_f, :] = hf
        q_s[0, idx_f, :] = pf
        ab = a_s[1, idx_b, :]
        hb = ab * hb + p_s[1, idx_b, :]
        pb = pb * ab
        h_s[1, idx_b, :] = hb
        q_s[1, idx_b, :] = pb
        return hf, pf, hb, pb

    zero = jnp.zeros((SUBLANES, LANES), F32)
    one = jnp.ones((SUBLANES, LANES), F32)
    hf, pf, hb, pb = lax.fori_loop(0, RG_SEG, scan, (zero, one, zero, one), unroll=8)

    cf = zero
    for s in range(1, SUBLANES):
        cf = jnp.where(sub == s, pltpu.roll(hf + pf * cf, 1, axis=0), cf)
    cr = zero
    for s in range(SUBLANES - 2, -1, -1):
        cr = jnp.where(sub == s, pltpu.roll(hb + pb * cr, SUBLANES - 1, axis=0), cr)

    def finish(j, carry):
        rows = grp(j)
        h = h_s[0, rows, :] + q_s[0, rows, :] * cf + h_s[1, rows, :] + q_s[1, rows, :] * cr
        o_ref[0, pl.ds(j, SUBLANES, stride=RG_SEG), :] = h * jax.nn.gelu(gi_s[rows, :])
        return carry

    lax.fori_loop(0, RG_SEG, finish, 0, unroll=8)


def _rglru_gate_weights(gate_w, gate_b):
    nchunk = D_RG // LANES
    bd = D_RG // RG_BLOCKS
    per = LANES // bd
    blocks = gate_w.reshape(4, nchunk, per, bd, bd)
    blocks = jnp.transpose(blocks, (1, 2, 3, 0, 4))
    same_block = jnp.eye(per, dtype=F32)[None, :, None, None, :, None]
    w = blocks[:, :, :, :, None, :] * same_block
    w = w.reshape(nchunk, LANES, 4 * LANES).astype(BF16)
    b = gate_b.reshape(4, nchunk, LANES).transpose(1, 0, 2).reshape(nchunk, 1, 4 * LANES)
    return w, b


def _rglru_mixer(main, b, conv_w, conv_b, gate_w, gate_b, lam):
    nchunk = D_RG // LANES
    wg, bg = _rglru_gate_weights(gate_w, gate_b)
    main3 = main.reshape(b, SEQ, D_MAIN)
    blk = lambda off: pl.BlockSpec((1, SEQ, LANES), lambda i, c, off=off: (i, 0, off // LANES + c))
    yc = pl.pallas_call(
        _rglru_kernel,
        grid=(b, nchunk),
        in_specs=[blk(MAIN_RG), blk(MAIN_RG_GATE),
                  pl.BlockSpec((conv_w.shape[0], LANES), lambda i, c: (0, c)),
                  pl.BlockSpec((1, LANES), lambda i, c: (0, c)),
                  pl.BlockSpec((1, LANES, 4 * LANES), lambda i, c: (c, 0, 0)),
                  pl.BlockSpec((1, 1, 4 * LANES), lambda i, c: (c, 0, 0)),
                  pl.BlockSpec((2, LANES), lambda i, c: (0, c))],
        out_specs=pl.BlockSpec((1, SEQ, LANES), lambda i, c: (i, 0, c)),
        out_shape=jax.ShapeDtypeStruct((b, SEQ, D_RG), F32),
        scratch_shapes=[pltpu.VMEM((SEQ + RG_HALO * SUBLANES, LANES), F32), pltpu.VMEM((SEQ, LANES), F32)]
                       + [pltpu.VMEM((2, SEQ, LANES), F32)] * 4,
        compiler_params=_params("parallel", "parallel"),
        name="rglru",
    )(main3, main3, conv_w, conv_b.reshape(1, D_RG), wg, bg, lam)
    return yc.reshape(b * SEQ, D_RG)


MERGE_TM = 512


def _layer_norm(z, g, b):
    mu = jnp.mean(z, axis=-1, keepdims=True)
    zc = z - mu
    var = jnp.mean(zc * zc, axis=-1, keepdims=True)
    return zc * lax.rsqrt(var + LN_EPS) * g + b


def _time_major(ref, dil, scratch):
    if dil == 1:
        return ref[0, 0]
    tiles = D_GROUP // LANES
    for r in range(dil):
        for h in range(tiles):
            scratch[h, pl.ds(r, MERGE_TM // dil, stride=dil), :] = ref[0, r, :, h * LANES:(h + 1) * LANES]
    return jnp.concatenate([scratch[h] for h in range(tiles)], axis=1)


def _merge_kernel(*refs, first_tiles):
    x_refs = refs[:len(refs) - 21]
    (ya_ref, o0_ref, o1_ref, o2_ref, l0_ref, l1_ref, l2_ref, yc_ref,
     wg_ref, bgate_ref, wa_ref, wb_ref, wc_ref, wo_ref, g_ref, b_ref, out_ref,
     s0, s1, s2, s3) = refs[len(x_refs):]
    if len(x_refs) == 1:
        x = x_refs[0][...]
    else:
        x = jnp.where(pl.program_id(0) < first_tiles, x_refs[0][...], x_refs[1][...])
    xb = x.astype(BF16)
    d0, d1, d2 = DILATIONS
    o0, l0 = _time_major(o0_ref, d0, s0), _time_major(l0_ref, d0, s0)
    o1, l1 = _time_major(o1_ref, d1, s0), _time_major(l1_ref, d1, s1)
    o2, l2 = _time_major(o2_ref, d2, s2), _time_major(l2_ref, d2, s3)
    lmax = jnp.maximum(jnp.maximum(l0, l1), l2)
    e0, e1, e2 = jnp.exp(l0 - lmax), jnp.exp(l1 - lmax), jnp.exp(l2 - lmax)
    yb = (e0 * o0 + e1 * o1 + e2 * o2) / (e0 + e1 + e2)
    branches = ((ya_ref[...], wa_ref), (yb, wb_ref), (yc_ref[...], wc_ref))
    mixed = None
    for j, (y, w_ref) in enumerate(branches):
        cols = slice(j * D_MODEL, (j + 1) * D_MODEL)
        gate = jax.nn.sigmoid(_dot(xb, wg_ref[:, cols]) + bgate_ref[:, cols])
        term = gate * _dot(y.astype(BF16), w_ref[...])
        mixed = term if mixed is None else mixed + term
    z = DN_ALPHA * x + _dot(mixed.astype(BF16), wo_ref[...])
    out_ref[...] = _layer_norm(z, g_ref[...], b_ref[...])


def _merge_and_project(xs, ya, att, yc, w_gate, b_gate, w_a, w_b, w_c, w_o, ln_g, ln_b):
    tiles = [x.shape[0] // MERGE_TM for x in xs]
    t = sum(tiles) * MERGE_TM
    tps = SEQ // MERGE_TM
    x_specs = [pl.BlockSpec((MERGE_TM, D_MODEL),
                            lambda i, off=sum(tiles[:k]), last=tiles[k] - 1: (jnp.clip(i - off, 0, last), 0))
               for k in range(len(xs))]
    rows = lambda w: pl.BlockSpec((MERGE_TM, w), lambda i: (i, 0))
    grp = lambda d: pl.BlockSpec((1, d, MERGE_TM // d, D_GROUP), lambda i: (i // tps, 0, i % tps, 0))
    const = lambda a: pl.BlockSpec(a.shape, lambda i: (0,) * a.ndim, pipeline_mode=pl.Buffered(1))
    (o0, l0), (o1, l1), (o2, l2) = att
    weights = (w_gate, b_gate.reshape(1, -1), w_a, w_b, w_c, w_o, ln_g.reshape(1, -1), ln_b.reshape(1, -1))
    return pl.pallas_call(
        functools.partial(_merge_kernel, first_tiles=tiles[0]),
        grid=(t // MERGE_TM,),
        in_specs=x_specs + [rows(D_HY)] + [grp(d) for d in DILATIONS] * 2 + [rows(D_RG)]
                 + [const(a) for a in weights],
        out_specs=rows(D_MODEL),
        out_shape=jax.ShapeDtypeStruct((t, D_MODEL), F32),
        scratch_shapes=[pltpu.VMEM((D_GROUP // LANES, MERGE_TM, LANES), F32)] * 4,
        compiler_params=_params("parallel"),
        name="merge_project_norm",
    )(*xs, ya, o0, o1, o2, l0, l1, l2, yc, *weights)


FFN_TM = 512
FFN_CF = 1536


def _ffn_kernel(x_ref, prev_ref, next_ref, wup_ref, cw_ref, cb_ref, wdn_ref, g_ref, b_ref, *out_refs,
                first_tiles):
    i = pl.program_id(0)
    tiles_per_seq = SEQ // FFN_TM
    x = x_ref[...]
    xb = x.astype(BF16)
    has_prev = (i % tiles_per_seq) != 0
    has_next = (i % tiles_per_seq) != tiles_per_seq - 1
    halo = jnp.concatenate([prev_ref[...], next_ref[...]], axis=0).astype(BF16)
    xcat = jnp.concatenate([xb, halo], axis=0)
    row = lax.broadcasted_iota(jnp.int32, (FFN_TM, FFN_CF), 0)
    acc = None
    for c in range(D_FF // FFN_CF):
        cols = slice(c * FFN_CF, (c + 1) * FFN_CF)
        ucols = slice(D_FF + c * FFN_CF, D_FF + (c + 1) * FFN_CF)
        hgx = _dot(xcat, wup_ref[:, cols])
        hg = hgx[:FFN_TM]
        before = jnp.where(has_prev, hgx[FFN_TM + SUBLANES - 1:FFN_TM + SUBLANES, :], 0.0)
        after = jnp.where(has_next, hgx[FFN_TM + SUBLANES:FFN_TM + SUBLANES + 1, :], 0.0)
        up = jnp.where(row == 0, before, pltpu.roll(hg, 1, axis=0))
        dn = jnp.where(row == FFN_TM - 1, after, pltpu.roll(hg, FFN_TM - 1, axis=0))
        w = cw_ref[:, cols]
        conv = up * w[0:1, :] + hg * w[1:2, :] + dn * w[2:3, :] + cb_ref[:, cols]
        act = jax.nn.gelu(conv) * _dot(xb, wup_ref[:, ucols])
        term = _dot(act.astype(BF16), wdn_ref[cols, :])
        acc = term if acc is None else acc + term
    y = _layer_norm(DN_ALPHA * x + acc, g_ref[...], b_ref[...])
    if len(out_refs) == 1:
        out_refs[0][...] = y
    else:
        @pl.when(i < first_tiles)
        def _():
            out_refs[0][...] = y

        @pl.when(i >= first_tiles)
        def _():
            out_refs[1][...] = y


def _ffn(x2d, w_up, conv_w, conv_b, w_down, ln_g, ln_b, split_rows=None):
    t = x2d.shape[0]
    bpt = FFN_TM // SUBLANES
    nblk = t // SUBLANES
    const = lambda a: pl.BlockSpec(a.shape, lambda i: (0,) * a.ndim, pipeline_mode=pl.Buffered(1))
    weights = (w_up, conv_w, conv_b.reshape(1, -1), w_down, ln_g.reshape(1, -1), ln_b.reshape(1, -1))
    if split_rows is None:
        first = t // FFN_TM
        out_specs = [pl.BlockSpec((FFN_TM, D_MODEL), lambda i: (i, 0))]
        out_shape = [jax.ShapeDtypeStruct((t, D_MODEL), F32)]
    else:
        first = split_rows // FFN_TM
        out_specs = [pl.BlockSpec((FFN_TM, D_MODEL), lambda i: (jnp.minimum(i, first - 1), 0)),
                     pl.BlockSpec((FFN_TM, D_MODEL), lambda i: (jnp.maximum(i - first, 0), 0))]
        out_shape = [jax.ShapeDtypeStruct((split_rows, D_MODEL), F32),
                     jax.ShapeDtypeStruct((t - split_rows, D_MODEL), F32)]
    outs = pl.pallas_call(
        functools.partial(_ffn_kernel, first_tiles=first),
        grid=(t // FFN_TM,),
        in_specs=[pl.BlockSpec((FFN_TM, D_MODEL), lambda i: (i, 0)),
                  pl.BlockSpec((SUBLANES, D_MODEL), lambda i: (jnp.maximum(i * bpt - 1, 0), 0)),
                  pl.BlockSpec((SUBLANES, D_MODEL), lambda i: (jnp.minimum((i + 1) * bpt, nblk - 1), 0))]
                 + [const(a) for a in weights],
        out_specs=out_specs,
        out_shape=out_shape,
        compiler_params=_params("arbitrary"),
        name="ffn_norm",
    )(x2d, x2d, x2d, *weights)
    return outs[0] if split_rows is None else outs


def _encoder_layer(xs, l, p, khat, split_rows=None):
    b = sum(x.shape[0] for x in xs) // SEQ
    main, *att = _input_projection(xs, p["w_in"][l])
    ya = _hyena_mixer(main, b, p["hy_conv_w"][l], p["hy_conv_b"][l], khat[l], p["hy_bias"][l])
    att_out = [_dilated_attention_group(a, dil) for a, dil in zip(att, DILATIONS)]
    yc = _rglru_mixer(main, b, p["rg_conv_w"][l], p["rg_conv_b"][l], p["rg_gate_w"][l], p["rg_gate_b"][l],
                      p["rg_lam"][l])
    x2d = _merge_and_project(xs, ya, att_out, yc,
                             p["w_gate"][l], p["b_gate"][l], p["w_br_a"][l], p["w_br_b"][l], p["w_br_c"][l],
                             p["w_o"][l], p["ln1_g"][l], p["ln1_b"][l])
    return _ffn(x2d, p["w_up"][l], p["ffn_conv_w"][l], p["ffn_conv_b"][l], p["w_down"][l],
                p["ln2_g"][l], p["ln2_b"][l], split_rows=split_rows)


def _trunk(x_a, x_b, p):
    khat = _hyena_filter_spectra(p["hy_filt_w1"], p["hy_filt_b1"], p["hy_filt_w2"], p["hy_filt_b2"],
                                 p["hy_filt_w3"], p["hy_filt_b3"], p["hy_filt_freq"])
    xs = (x_a.reshape(-1, D_MODEL), x_b.reshape(-1, D_MODEL))
    rows_a = xs[0].shape[0]
    for l in range(DEPTH):
        split = rows_a if l == DEPTH - 1 else None
        out = _encoder_layer(xs, l, p, khat, split_rows=split)
        xs = tuple(out) if split is not None else (out,)
    return xs[0].reshape(x_a.shape), xs[1].reshape(x_b.shape)


_MATMUL_WEIGHTS = ("w_in", "w_gate", "w_br_a", "w_br_b", "w_br_c", "w_o", "w_up", "w_down")


def kernel(x_prompt, x_sample, w_in, hy_conv_w, hy_conv_b, hy_filt_w1, hy_filt_b1, hy_filt_w2, hy_filt_b2, hy_filt_w3, hy_filt_b3, hy_filt_freq, hy_bias, rg_conv_w, rg_conv_b, rg_gate_w, rg_gate_b, rg_lam, w_gate, b_gate, w_br_a, w_br_b, w_br_c, w_o, ln1_g, ln1_b, w_up, ffn_conv_w, ffn_conv_b, w_down, ln2_g, ln2_b):
    p = dict(w_in=w_in, hy_conv_w=hy_conv_w, hy_conv_b=hy_conv_b, hy_filt_w1=hy_filt_w1,
             hy_filt_b1=hy_filt_b1, hy_filt_w2=hy_filt_w2, hy_filt_b2=hy_filt_b2, hy_filt_w3=hy_filt_w3,
             hy_filt_b3=hy_filt_b3, hy_filt_freq=hy_filt_freq, hy_bias=hy_bias, rg_conv_w=rg_conv_w,
             rg_conv_b=rg_conv_b, rg_gate_w=rg_gate_w, rg_gate_b=rg_gate_b, rg_lam=rg_lam, w_gate=w_gate,
             b_gate=b_gate, w_br_a=w_br_a, w_br_b=w_br_b, w_br_c=w_br_c, w_o=w_o, ln1_g=ln1_g, ln1_b=ln1_b,
             w_up=w_up, ffn_conv_w=ffn_conv_w, ffn_conv_b=ffn_conv_b, w_down=w_down, ln2_g=ln2_g,
             ln2_b=ln2_b)
    for name in _MATMUL_WEIGHTS:
        p[name] = p[name].astype(BF16)
    return _trunk(x_prompt, x_sample, p)
```

```python
import functools
import math

import numpy as np
import jax
import jax.numpy as jnp
from jax import lax
from jax.experimental import pallas as pl
from jax.experimental.pallas import tpu as pltpu

F32 = jnp.float32
BF16 = jnp.bfloat16

D_MODEL = 1024
SEQ = 4096
DEPTH = 2
D_HY = 512
HY_EMB_BANDS = 8
HY_EMB_PAD = 32
HY_FAST_DECAY = 0.3
HY_SLOW_DECAY = 1.5
HY_TARGET = 1e-2
HEAD_DIM = 64
HEADS_PER_GROUP = 4
DILATIONS = (1, 4, 16)
RADIUS = 64
D_GROUP = HEADS_PER_GROUP * HEAD_DIM
D_ATT = len(DILATIONS) * D_GROUP
ROPE_THETA = 10000.0
NEG_INF = -1e30
DEN_FLOOR = 1e-30
D_RG = 512
RG_BLOCKS = 8
RG_C = 8.0
D_FF = 3 * D_MODEL
DN_ALPHA = (2 * DEPTH) ** 0.25
LN_EPS = 1e-5
D_IN = 3 * D_HY + 3 * D_ATT + 2 * D_RG
OFF_Q = 3 * D_HY
OFF_K = OFF_Q + D_ATT
OFF_V = OFF_K + D_ATT
OFF_RG = OFF_V + D_ATT
D_MAIN = 2 * D_HY + 2 * D_RG
MAIN_RG = 2 * D_HY
MAIN_RG_GATE = MAIN_RG + D_RG

LANES = 128
SUBLANES = 8
VMEM_LIMIT = 56 * 1024 * 1024

FFT_N = 2 * SEQ
FFT_N1 = 64
FFT_N2 = 128
FFT_HALF = FFT_N1 // 2
HY_COLS = FFT_N2 * D_HY
HY_SLABS = FFT_N1 // 2 + 1
HY_SLABS_PAD = 40


def _params(*sem):
    return pltpu.CompilerParams(dimension_semantics=sem, vmem_limit_bytes=VMEM_LIMIT)


def _dot(a, b):
    return jnp.dot(a, b, preferred_element_type=F32)


def _cos_sin(a, b, period):
    ang = 2.0 * np.pi * (np.outer(a, b) % period) / period
    return np.cos(ang), np.sin(ang)


@functools.lru_cache(maxsize=None)
def _dft_constants():
    n2 = np.arange(FFT_N2)
    k2 = np.arange(FFT_N2)
    g = np.zeros((HY_SLABS_PAD, 2 * FFT_N2, 2 * FFT_N2), np.float64)
    for a in range(HY_SLABS_PAD):
        c, s = _cos_sin(a + FFT_N1 * k2, n2, FFT_N)
        gr, gi = c, -s
        g[a] = np.block([[gr, -gi], [gi, gr]])
    c, s = _cos_sin(np.arange(HY_SLABS_PAD), np.arange(FFT_N1), FFT_N1)
    m1_full = np.kron(np.concatenate([c, -s], axis=0), np.eye(SUBLANES))
    return jnp.asarray(m1_full, BF16), jnp.asarray(g, BF16)


@functools.lru_cache(maxsize=None)
def _hyena_stage_constants():
    k1 = np.arange(HY_SLABS_PAD)
    n1 = np.arange(FFT_HALF)
    eye = np.eye(SUBLANES)
    c, s = _cos_sin(k1, n1, FFT_N1)
    m1 = np.kron(np.concatenate([c, -s], axis=0), eye)
    wgt = np.where((k1 == 0) | (k1 == FFT_N1 // 2), 1.0, 2.0) * (k1 < HY_SLABS)
    c, s = _cos_sin(n1, k1, FFT_N1)
    m3 = np.kron(np.concatenate([c * wgt, -s * wgt], axis=1) / FFT_N, eye)
    return jnp.asarray(m1, BF16), jnp.asarray(m3, BF16)


@functools.lru_cache(maxsize=None)
def _filter_positions():
    L = SEQ
    t = np.linspace(0.0, 1.0, L, dtype=np.float32).astype(np.float64)[:, None]
    w = (2.0 * np.pi * np.arange(L, dtype=np.float32) / L).astype(np.float64)[:, None]
    bands = np.linspace(1e-4, HY_EMB_BANDS - 1, HY_EMB_BANDS, dtype=np.float32).astype(np.float64)[None, :]
    z = np.concatenate([t, np.cos(bands * w), -np.sin(bands * w)], axis=-1)
    m = np.arange(2 * L)
    src = np.where(m < L, m, 2 * L - m)
    src = np.where(m == L, 0, src)
    zp = np.zeros((2 * L, HY_EMB_PAD))
    zp[:, :z.shape[1]] = z[src]
    return jnp.asarray(zp, F32)


@functools.lru_cache(maxsize=None)
def _rope_tables():
    inv = ROPE_THETA ** (-np.arange(0, HEAD_DIM, 2, dtype=np.float32).astype(np.float64) / HEAD_DIM)
    ang = np.arange(SEQ, dtype=np.float64)[:, None] * inv[None, :]
    cos = np.concatenate([np.cos(ang), np.cos(ang)], axis=1)
    sin = np.concatenate([-np.sin(ang), np.sin(ang)], axis=1)
    cos = np.tile(cos, (1, HEADS_PER_GROUP))
    sin = np.tile(sin, (1, HEADS_PER_GROUP))
    return jnp.asarray(cos, F32), jnp.asarray(sin, F32)


PROJ_TM = 512
_MAIN_CHUNKS = tuple((OFF_RG + j, MAIN_RG + j, 512) for j in range(0, 2 * D_RG, 512))


def _rope(t, cos, sin):
    lane = lax.broadcasted_iota(jnp.int32, t.shape, 1)
    first_half = (lane % HEAD_DIM) < (HEAD_DIM // 2)
    rot = jnp.where(first_half, pltpu.roll(t, D_GROUP - HEAD_DIM // 2, axis=1),
                    pltpu.roll(t, HEAD_DIM // 2, axis=1))
    return t * cos + rot * sin


def _proj_kernel(*refs, first_tiles):
    x_refs = refs[:len(refs) - 10]
    w_ref, cw_ref, cb_ref, cos_ref, sin_ref, main_ref, a0_ref, a1_ref, a2_ref, qkv_s = refs[len(x_refs):]
    i = pl.program_id(0)
    if len(x_refs) == 3:
        x, prev8, next8 = (r[...] for r in x_refs)
    else:
        x, prev8, next8 = (jnp.where(i < first_tiles, a[...], b[...]) for a, b in zip(x_refs[:3], x_refs[3:]))
    xb = x.astype(BF16)
    for src, dst, width in _MAIN_CHUNKS:
        main_ref[:, dst:dst + width] = _dot(xb, w_ref[:, src:src + width])

    tps = SEQ // PROJ_TM
    has_prev = (i % tps) != 0
    has_next = (i % tps) != tps - 1
    xcat = jnp.concatenate([xb, prev8.astype(BF16), next8.astype(BF16)], axis=0)
    row = lax.broadcasted_iota(jnp.int32, (PROJ_TM, D_HY), 0)
    parts = []
    for part in range(3):
        cols = slice(part * D_HY, (part + 1) * D_HY)
        hx = _dot(xcat, w_ref[:, cols])
        h = hx[:PROJ_TM]
        before = jnp.where(has_prev, hx[PROJ_TM + SUBLANES - 1:PROJ_TM + SUBLANES, :], 0.0)
        after = jnp.where(has_next, hx[PROJ_TM + SUBLANES:PROJ_TM + SUBLANES + 1, :], 0.0)
        up = jnp.where(row == 0, before, pltpu.roll(h, 1, axis=0))
        dn = jnp.where(row == PROJ_TM - 1, after, pltpu.roll(h, PROJ_TM - 1, axis=0))
        w = cw_ref[:, cols]
        parts.append(up * w[0:1, :] + h * w[1:2, :] + dn * w[2:3, :] + cb_ref[:, cols])
    main_ref[:, 0:D_HY] = parts[0]
    main_ref[:, D_HY:2 * D_HY] = parts[2] * parts[1]
    cos = cos_ref[...]
    sin = sin_ref[...]
    for g, (dil, att_ref) in enumerate(zip(DILATIONS, (a0_ref, a1_ref, a2_ref))):
        col = lambda off: slice(off + g * D_GROUP, off + (g + 1) * D_GROUP)
        q = _rope(_dot(xb, w_ref[:, col(OFF_Q)]), cos, sin) * (HEAD_DIM ** -0.5)
        k = _rope(_dot(xb, w_ref[:, col(OFF_K)]), cos, sin)
        v = _dot(xb, w_ref[:, col(OFF_V)])
        if dil == 1:
            att_ref[0, 0, :, 0:D_GROUP] = q
            att_ref[0, 0, :, D_GROUP:2 * D_GROUP] = k
            att_ref[0, 0, :, 2 * D_GROUP:] = v
        else:
            for j, t in enumerate((q, k, v)):
                for h in range(D_GROUP // LANES):
                    qkv_s[j * (D_GROUP // LANES) + h] = t[:, h * LANES:(h + 1) * LANES]
            for r in range(dil):
                for j in range(3 * D_GROUP // LANES):
                    att_ref[0, r, :, j * LANES:(j + 1) * LANES] = qkv_s[j, pl.ds(r, PROJ_TM // dil, stride=dil), :]


def _input_projection(xs, w_in_bf16, hy_conv_w, hy_conv_b):
    tiles = [x.shape[0] // PROJ_TM for x in xs]
    t = sum(tiles) * PROJ_TM
    b = t // SEQ
    tps = SEQ // PROJ_TM
    bpt = PROJ_TM // SUBLANES
    cos, sin = _rope_tables()
    tab = pl.BlockSpec((PROJ_TM, D_GROUP), lambda i: (i % tps, 0))
    att_spec = lambda d: pl.BlockSpec((1, d, PROJ_TM // d, 3 * D_GROUP), lambda i: (i // tps, 0, i % tps, 0))
    att_shape = lambda d: jax.ShapeDtypeStruct((b, d, SEQ // d, 3 * D_GROUP), F32)
    x_specs, x_args = [], []
    for k, x in enumerate(xs):
        tile = lambda i, off=sum(tiles[:k]), last=tiles[k] - 1: jnp.clip(i - off, 0, last)
        nblk = x.shape[0] // SUBLANES
        x_specs += [pl.BlockSpec((PROJ_TM, D_MODEL), lambda i, tile=tile: (tile(i), 0)),
                    pl.BlockSpec((SUBLANES, D_MODEL), lambda i, tile=tile: (jnp.maximum(tile(i) * bpt - 1, 0), 0)),
                    pl.BlockSpec((SUBLANES, D_MODEL),
                                 lambda i, tile=tile, nblk=nblk: (jnp.minimum((tile(i) + 1) * bpt, nblk - 1), 0))]
        x_args += [x, x, x]
    full = lambda a: pl.BlockSpec(a.shape, lambda i: (0,) * a.ndim)
    conv_b = hy_conv_b.reshape(1, -1)
    return pl.pallas_call(
        functools.partial(_proj_kernel, first_tiles=tiles[0]),
        grid=(t // PROJ_TM,),
        in_specs=x_specs + [pl.BlockSpec((D_MODEL, D_IN), lambda i: (0, 0), pipeline_mode=pl.Buffered(1)),
                            full(hy_conv_w), full(conv_b), tab, tab],
        out_specs=[pl.BlockSpec((PROJ_TM, D_MAIN), lambda i: (i, 0))] + [att_spec(d) for d in DILATIONS],
        out_shape=[jax.ShapeDtypeStruct((t, D_MAIN), F32)] + [att_shape(d) for d in DILATIONS],
        scratch_shapes=[pltpu.VMEM((3 * D_GROUP // LANES, PROJ_TM, LANES), F32)],
        compiler_params=_params("parallel"),
        name="input_projection",
    )(*x_args, w_in_bf16, hy_conv_w, conv_b, cos, sin)


FILT_ROWS = 512


def _hy_filter_kernel(z_ref, w1_ref, b1_ref, w2_ref, b2_ref, w3f_ref, w3b_ref, b3f_ref, b3b_ref,
                      fr_ref, dl_ref, m1_ref, g_ref, kh_ref, h_s, k_s, a_s):
    hi = lax.Precision.HIGHEST
    w3f, w3b, b3f, b3b = w3f_ref[0], w3b_ref[0], b3f_ref[0], b3b_ref[0]
    delta = dl_ref[...]

    @pl.when(pl.program_id(1) == 0)
    def _():
        fr = fr_ref[0]
        w1, b1, w2, b2 = w1_ref[0], b1_ref[0], w2_ref[0], b2_ref[0]

        def hidden(i, carry):
            rows = pl.ds(pl.multiple_of(i * FILT_ROWS, FILT_ROWS), FILT_ROWS)
            h = jnp.sin(fr * (jnp.dot(z_ref[rows, :], w1, precision=hi, preferred_element_type=F32) + b1))
            h_s[rows, :] = jnp.sin(fr * (jnp.dot(h, w2, precision=hi, preferred_element_type=F32) + b2))
            return carry

        lax.fori_loop(0, FFT_N // FILT_ROWS, hidden, 0)

    def body(i, total):
        r0 = pl.multiple_of(i * FILT_ROWS, FILT_ROWS)
        z = z_ref[pl.ds(r0, FILT_ROWS), :]
        h = h_s[pl.ds(r0, FILT_ROWS), :]
        hf = jnp.dot(h, w3f, precision=hi, preferred_element_type=F32) + b3f
        hb = jnp.dot(h, w3b, precision=hi, preferred_element_type=F32) + b3b
        m = lax.broadcasted_iota(jnp.int32, hf.shape, 0) + r0
        decay = jnp.exp(-z[:, 0:1] * delta)
        val = jnp.where(m < SEQ, hf, hb) * decay
        val = jnp.where(m == SEQ, 0.0, val)
        k_s[pl.ds(r0, FILT_ROWS), :] = val
        return total + jnp.sum(jnp.abs(val), axis=0, keepdims=True)

    total = lax.fori_loop(0, FFT_N // FILT_ROWS, body, jnp.zeros((1, LANES), F32))

    def scale(i, carry):
        r0 = pl.multiple_of(i * FILT_ROWS, FILT_ROWS)
        k_s[pl.ds(r0, FILT_ROWS), :] = k_s[pl.ds(r0, FILT_ROWS), :] / total
        return carry

    lax.fori_loop(0, FFT_N // FILT_ROWS, scale, 0)

    def stage1(j, carry):
        n2 = pl.multiple_of(j * SUBLANES, SUBLANES)
        x = jnp.concatenate([k_s[pl.ds(n1 * FFT_N2 + n2, SUBLANES), :] for n1 in range(FFT_N1)], axis=0)
        a = _dot(m1_ref[...], x.astype(BF16)).reshape(2 * HY_SLABS_PAD, SUBLANES, LANES)
        a_s[0, :, pl.ds(n2, SUBLANES), :] = a[:HY_SLABS_PAD]
        a_s[1, :, pl.ds(n2, SUBLANES), :] = a[HY_SLABS_PAD:]
        return carry

    lax.fori_loop(0, FFT_N2 // SUBLANES, stage1, 0)

    def stage2(k1, carry):
        a = jnp.concatenate([a_s[0, k1], a_s[1, k1]], axis=0).astype(BF16)
        x = _dot(g_ref[k1], a)
        rows = pl.ds(pl.multiple_of(k1 * FFT_N2, FFT_N2), FFT_N2)
        kh_ref[0, 0, rows, :] = x[:FFT_N2]
        kh_ref[0, 1, rows, :] = x[FFT_N2:]
        return carry

    lax.fori_loop(0, HY_SLABS, stage2, 0)


def _hyena_filter_spectra(w1, b1, w2, b2, w3, b3, freq):
    nchunk = D_HY // LANES
    m1_full, g = _dft_constants()
    z = _filter_positions()
    emb = z.shape[1]
    w1 = jnp.pad(w1, ((0, 0), (0, emb - w1.shape[1]), (0, 0)))
    hid = w1.shape[-1]
    deltas = np.abs(np.linspace(math.log(HY_TARGET) / HY_SLOW_DECAY, math.log(HY_TARGET) / HY_FAST_DECAY,
                                D_HY, dtype=np.float32))
    deltas = jnp.asarray(deltas, F32).reshape(1, D_HY)
    row = lambda a: a.reshape(DEPTH, 1, a.shape[-1])
    full = lambda s: pl.BlockSpec((1,) + s, lambda l, c: (l, 0, 0))
    return pl.pallas_call(
        _hy_filter_kernel,
        grid=(DEPTH, nchunk),
        in_specs=[pl.BlockSpec((FFT_N, emb), lambda l, c: (0, 0)),
                  full((emb, hid)), full((1, hid)), full((hid, hid)), full((1, hid)),
                  pl.BlockSpec((1, hid, LANES), lambda l, c: (l, 0, c)),
                  pl.BlockSpec((1, hid, LANES), lambda l, c: (l, 0, nchunk + c)),
                  pl.BlockSpec((1, 1, LANES), lambda l, c: (l, 0, c)),
                  pl.BlockSpec((1, 1, LANES), lambda l, c: (l, 0, nchunk + c)),
                  full((1, hid)),
                  pl.BlockSpec((1, LANES), lambda l, c: (0, c)),
                  pl.BlockSpec(m1_full.shape, lambda l, c: (0, 0), pipeline_mode=pl.Buffered(1)),
                  pl.BlockSpec(g.shape, lambda l, c: (0, 0, 0), pipeline_mode=pl.Buffered(1))],
        out_specs=pl.BlockSpec((1, 2, HY_SLABS * FFT_N2, LANES), lambda l, c: (l, 0, 0, c)),
        out_shape=jax.ShapeDtypeStruct((DEPTH, 2, HY_SLABS * FFT_N2, D_HY), F32),
        scratch_shapes=[pltpu.VMEM((FFT_N, hid), F32), pltpu.VMEM((FFT_N, LANES), F32),
                        pltpu.VMEM((2, HY_SLABS_PAD, FFT_N2, LANES), F32)],
        compiler_params=_params("parallel", "arbitrary"),
        name="hyena_filter",
    )(z, w1, row(b1), w2, row(b2), w3, w3, row(b3), row(b3), row(freq), deltas, m1_full, g)


HY_SLAB_UNROLL = 11
HY_EDGE_UNROLL = 4
HY_PAIR = 2


def _hyena_kernel(x0_ref, uu_ref, m1_ref, m3_ref, g_ref, kh_ref, bias_ref, o_ref, a_s):
    width = HY_PAIR * LANES
    side_by_side = lambda ref, n2: jnp.concatenate(
        [ref[s, :, n2, :].reshape(FFT_HALF * SUBLANES, LANES) for s in range(HY_PAIR)], axis=1)

    def stage1(j, carry):
        blocks = [pl.ds(pl.multiple_of((j * HY_EDGE_UNROLL + u) * SUBLANES, SUBLANES), SUBLANES)
                  for u in range(HY_EDGE_UNROLL)]
        xs = [side_by_side(uu_ref, n2).astype(BF16) for n2 in blocks]
        res = [_dot(m1_ref[...], xb).reshape(2 * HY_SLABS_PAD, SUBLANES, width) for xb in xs]
        for n2, a in zip(blocks, res):
            a_s[0, :, n2, :] = a[:HY_SLABS_PAD]
            a_s[1, :, n2, :] = a[HY_SLABS_PAD:]
        return carry

    lax.fori_loop(0, FFT_N2 // SUBLANES // HY_EDGE_UNROLL, stage1, 0)

    def stage2(j, carry):
        slabs = [j * HY_SLAB_UNROLL + u for u in range(HY_SLAB_UNROLL)]
        gs = [g_ref[k1] for k1 in slabs]
        xs = [jnp.concatenate([a_s[0, k1], a_s[1, k1]], axis=0).astype(BF16) for k1 in slabs]
        res = []
        for k1, g, a in zip(slabs, gs, xs):
            x = _dot(g, a)
            xr, xi = x[:FFT_N2], x[FFT_N2:]
            rows = pl.ds(pl.multiple_of(k1 * FFT_N2, FFT_N2), FFT_N2)
            kr = jnp.concatenate([kh_ref[0, rows, :]] * HY_PAIR, axis=1)
            ki = jnp.concatenate([kh_ref[1, rows, :]] * HY_PAIR, axis=1)
            z = jnp.concatenate([xr * kr - xi * ki, xr * ki + xi * kr], axis=0).astype(BF16)
            res.append(lax.dot_general(g, z, (((0,), (0,)), ((), ())), preferred_element_type=F32))
        for k1, d in zip(slabs, res):
            a_s[0, k1] = d[:FFT_N2]
            a_s[1, k1] = d[FFT_N2:]
        return carry

    lax.fori_loop(0, HY_SLABS // HY_SLAB_UNROLL, stage2, 0)

    bias = bias_ref[...]

    def stage3(j, carry):
        blocks = [pl.ds(pl.multiple_of((j * HY_EDGE_UNROLL + u) * SUBLANES, SUBLANES), SUBLANES)
                  for u in range(HY_EDGE_UNROLL)]
        ds = [jnp.concatenate([a_s[0, :, n2, :], a_s[1, :, n2, :]], axis=0)
              .reshape(2 * HY_SLABS_PAD * SUBLANES, width).astype(BF16) for n2 in blocks]
        ys = [_dot(m3_ref[...], d).reshape(FFT_HALF, SUBLANES, width) for d in ds]
        for n2, y in zip(blocks, ys):
            for s in range(HY_PAIR):
                ys_s = y[:, :, s * LANES:(s + 1) * LANES]
                o_ref[s, :, n2, :] = x0_ref[s, :, n2, :] * (ys_s + uu_ref[s, :, n2, :] * bias)
        return carry

    lax.fori_loop(0, FFT_N2 // SUBLANES // HY_EDGE_UNROLL, stage3, 0)


def _hyena_mixer(main, b, khat, hy_bias):
    assert b % HY_PAIR == 0
    nchunk = D_HY // LANES
    m1, m3 = _hyena_stage_constants()
    _, g = _dft_constants()
    main4 = main.reshape(b, FFT_HALF, FFT_N2, D_MAIN)
    blk = lambda part: pl.BlockSpec((HY_PAIR, FFT_HALF, FFT_N2, LANES),
                                    lambda c, i, part=part: (i, 0, 0, part * nchunk + c))
    const = lambda a: pl.BlockSpec(a.shape, lambda c, i: (0,) * a.ndim, pipeline_mode=pl.Buffered(1))
    rows = HY_SLABS * FFT_N2
    ya = pl.pallas_call(
        _hyena_kernel,
        grid=(nchunk, b // HY_PAIR),
        in_specs=[blk(0), blk(1),
                  const(m1), const(m3),
                  pl.BlockSpec((HY_SLABS_PAD, 2 * FFT_N2, 2 * FFT_N2), lambda c, i: (0, 0, 0),
                               pipeline_mode=pl.Buffered(1)),
                  pl.BlockSpec((2, rows, LANES), lambda c, i: (0, 0, c), pipeline_mode=pl.Buffered(1)),
                  pl.BlockSpec((1, LANES), lambda c, i: (0, c))],
        out_specs=pl.BlockSpec((HY_PAIR, FFT_HALF, FFT_N2, LANES), lambda c, i: (i, 0, 0, c)),
        out_shape=jax.ShapeDtypeStruct((b, FFT_HALF, FFT_N2, D_HY), F32),
        scratch_shapes=[pltpu.VMEM((2, HY_SLABS_PAD, FFT_N2, HY_PAIR * LANES), F32)],
        compiler_params=_params("parallel", "parallel"),
        name="hyena",
    )(main4, main4, m1, m3, g, khat, hy_bias.reshape(1, D_HY))
    return ya.reshape(b * SEQ, D_HY)


ATT_TQ = 128
ATT_WIN = ATT_TQ + 2 * RADIUS
ATT_QC = 1024
ATT_UNROLL = 4
ATT_MAX_RESIDUE_STEPS = 4


def _attn_kernel(q_ref, kp_ref, k_ref, kn_ref, vp_ref, v_ref, vn_ref, o_ref, lse_ref, qs, ks, vs, *, n, qc, rpb):
    c0 = pl.program_id(2) * qc
    lane = lax.broadcasted_iota(jnp.int32, (qc, D_GROUP), 1)
    even_head = (lane % LANES) < HEAD_DIM
    lane_p = lax.broadcasted_iota(jnp.int32, (ATT_TQ, LANES), 1)
    low_head = lane_p < HEAD_DIM
    row = lax.broadcasted_iota(jnp.int32, (ATT_TQ, ATT_WIN), 0)
    col = lax.broadcasted_iota(jnp.int32, (ATT_TQ, ATT_WIN), 1)
    band = jnp.abs(col - RADIUS - row) - RADIUS
    unroll = min(ATT_UNROLL, qc // ATT_TQ)

    def residue(rr, carry):
        q = q_ref[0, rr]
        qs[0] = jnp.where(even_head, q, 0.0).astype(BF16)
        qs[1] = jnp.where(even_head, 0.0, q).astype(BF16)
        ks[0:RADIUS] = kp_ref[0, rr].astype(BF16)
        ks[RADIUS:RADIUS + qc] = k_ref[0, rr].astype(BF16)
        ks[RADIUS + qc:] = kn_ref[0, rr].astype(BF16)
        vs[0:RADIUS] = vp_ref[0, rr].astype(BF16)
        vs[RADIUS:RADIUS + qc] = v_ref[0, rr].astype(BF16)
        vs[RADIUS + qc:] = vn_ref[0, rr].astype(BF16)

        def query_block(q0):
            pos = col + (c0 + q0 - RADIUS)
            mask = jnp.maximum(jnp.maximum(band, -pos), pos - (n - 1)) <= 0
            for hp in range(D_GROUP // LANES):
                ls = slice(hp * LANES, (hp + 1) * LANES)
                kp = ks[pl.ds(q0, ATT_WIN), ls]
                vp = vs[pl.ds(q0, ATT_WIN), ls]
                outs, lses = [], []
                for h in range(2):
                    s = lax.dot_general(qs[h, pl.ds(q0, ATT_TQ), ls], kp, (((1,), (1,)), ((), ())),
                                        preferred_element_type=F32)
                    s = jnp.where(mask, s, NEG_INF)
                    m = jnp.max(s, axis=-1, keepdims=True)
                    e = jnp.where(mask, jnp.exp(s - m), 0.0)
                    den = jnp.maximum(jnp.sum(e, axis=-1, keepdims=True), DEN_FLOOR)
                    outs.append(_dot(e.astype(BF16), vp) / den)
                    lses.append(m + jnp.log(den))
                o_ref[0, rr, pl.ds(q0, ATT_TQ), ls] = jnp.where(low_head, outs[0], outs[1])
                lse_ref[0, rr, pl.ds(q0, ATT_TQ), ls] = jnp.where(low_head, lses[0], lses[1])

        def body(i, inner):
            for u in range(unroll):
                query_block(pl.multiple_of((i * unroll + u) * ATT_TQ, ATT_TQ))
            return inner

        lax.fori_loop(0, qc // ATT_TQ // unroll, body, 0)
        return carry

    lax.fori_loop(0, rpb, residue, 0)


def _dilated_attention_group(att, dil):
    b = att.shape[0]
    n = SEQ // dil
    qc = min(n, ATT_QC)
    hpc = qc // RADIUS
    last = n // RADIUS - 1
    rpb = max(1, dil // ATT_MAX_RESIDUE_STEPS)

    def halo_specs(col):
        return [pl.BlockSpec((1, rpb, RADIUS, D_GROUP), lambda i, r, c: (i, r, jnp.maximum(c * hpc - 1, 0), col)),
                pl.BlockSpec((1, rpb, qc, D_GROUP), lambda i, r, c: (i, r, c, col)),
                pl.BlockSpec((1, rpb, RADIUS, D_GROUP), lambda i, r, c: (i, r, jnp.minimum((c + 1) * hpc, last), col))]

    out = pl.BlockSpec((1, rpb, qc, D_GROUP), lambda i, r, c: (i, r, c, 0))
    return pl.pallas_call(
        functools.partial(_attn_kernel, n=n, qc=qc, rpb=rpb),
        grid=(b, dil // rpb, n // qc),
        in_specs=[halo_specs(0)[1]] + halo_specs(1) + halo_specs(2),
        out_specs=[out, out],
        out_shape=[jax.ShapeDtypeStruct((b, dil, n, D_GROUP), F32)] * 2,
        scratch_shapes=[pltpu.VMEM((2, qc, D_GROUP), BF16), pltpu.VMEM((qc + 2 * RADIUS, D_GROUP), BF16),
                        pltpu.VMEM((qc + 2 * RADIUS, D_GROUP), BF16)],
        compiler_params=_params("parallel", "parallel", "parallel"),
        name=f"dilated_attention_d{dil}",
    )(att, att, att, att, att, att, att)


RG_ROWS = 512
RG_SEG = SEQ // SUBLANES
RG_HALO = 3


def _rglru_kernel(xr_ref, gate_ref, cw_ref, cb_ref, wg_ref, bg_ref, lam_ref, o_ref,
                  xe_s, gi_s, a_s, p_s, h_s, q_s):
    grp = lambda j: pl.ds(pl.multiple_of(j * SUBLANES, SUBLANES), SUBLANES)

    def gather(j, carry):
        src = pl.ds(j, SUBLANES, stride=RG_SEG)
        xe_s[grp(j + 2), :] = xr_ref[0, src, :]
        gi_s[grp(j), :] = gate_ref[0, src, :]
        return carry

    lax.fori_loop(0, RG_SEG, gather, 0, unroll=8)

    sub = lax.broadcasted_iota(jnp.int32, (SUBLANES, LANES), 0)
    vreg = lambda j: slice((j + 2) * SUBLANES, (j + 3) * SUBLANES)
    for j in (-2, -1):
        xe_s[vreg(j), :] = jnp.where(sub == 0, 0.0, pltpu.roll(xe_s[vreg(RG_SEG + j), :], 1, axis=0))
    xe_s[vreg(RG_SEG), :] = jnp.where(sub == SUBLANES - 1, 0.0,
                                      pltpu.roll(xe_s[vreg(0), :], SUBLANES - 1, axis=0))

    lam = lam_ref[...]
    softplus = jnp.maximum(-lam, 0.0) + jnp.log1p(jnp.exp(-jnp.abs(lam)))
    log_a_scale = -RG_C * softplus
    wg = wg_ref[0]
    bg = bg_ref[0]
    cw = cw_ref[...]
    cb = cb_ref[...]

    def gates(i, carry):
        r0 = pl.multiple_of(i * RG_ROWS, RG_ROWS)
        xr = cb
        for k in range(RG_HALO + 1):
            xr = xr + cw[k:k + 1, :] * xe_s[pl.ds(r0 + k * SUBLANES, RG_ROWS), :]
        g = _dot(xr.astype(BF16), wg) + bg
        for d in range(2):
            r = 0.5 + 0.5 * jnp.tanh(0.5 * g[:, (2 * d) * LANES:(2 * d + 1) * LANES])
            ig = 0.5 + 0.5 * jnp.tanh(0.5 * g[:, (2 * d + 1) * LANES:(2 * d + 2) * LANES])
            log_a = r * log_a_scale[d:d + 1, :]
            a = jnp.exp(log_a)
            t = jnp.tanh(log_a)
            mult = jnp.sqrt(-2.0 * t / (1.0 - t))
            xn = xr * ig
            a_s[d, pl.ds(r0, RG_ROWS), :] = a
            p_s[d, pl.ds(r0, RG_ROWS), :] = xn * mult
            edge = slice(0, SUBLANES) if d == 0 else slice(RG_ROWS - SUBLANES, RG_ROWS)
            at_start = sub == (0 if d == 0 else SUBLANES - 1)

            @pl.when(i == (0 if d == 0 else SEQ // RG_ROWS - 1))
            def _():
                p_s[d, pl.ds(r0 + edge.start, SUBLANES), :] = xn[edge] * jnp.where(at_start, 1.0, mult[edge])
        return carry

    lax.fori_loop(0, SEQ // RG_ROWS, gates, 0)

    def scan(j, carry):
        hf, pf, hb, pb = carry
        idx_f = grp(j)
        idx_b = grp(RG_SEG - 1 - j)
        af = a_s[0, idx_f, :]
        hf = af * hf + p_s[0, idx_f, :]
        pf = pf * af
        h_s[0, idx_f, :] = hf
        q_s[0, idx_f, :] = pf
        ab = a_s[1, idx_b, :]
        hb = ab * hb + p_s[1, idx_b, :]
        pb = pb * ab
        h_s[1, idx_b, :] = hb
        q_s[1, idx_b, :] = pb
        return hf, pf, hb, pb

    zero = jnp.zeros((SUBLANES, LANES), F32)
    one = jnp.ones((SUBLANES, LANES), F32)
    hf, pf, hb, pb = lax.fori_loop(0, RG_SEG, scan, (zero, one, zero, one), unroll=8)

    cf = zero
    for s in range(1, SUBLANES):
        cf = jnp.where(sub == s, pltpu.roll(hf + pf * cf, 1, axis=0), cf)
    cr = zero
    for s in range(SUBLANES - 2, -1, -1):
        cr = jnp.where(sub == s, pltpu.roll(hb + pb * cr, SUBLANES - 1, axis=0), cr)

    def finish(j, carry):
        rows = grp(j)
        h = h_s[0, rows, :] + q_s[0, rows, :] * cf + h_s[1, rows, :] + q_s[1, rows, :] * cr
        o_ref[0, pl.ds(j, SUBLANES, stride=RG_SEG), :] = h * jax.nn.gelu(gi_s[rows, :])
        return carry

    lax.fori_loop(0, RG_SEG, finish, 0, unroll=8)


def _rglru_gate_weights(gate_w, gate_b):
    nchunk = D_RG // LANES
    bd = D_RG // RG_BLOCKS
    per = LANES // bd
    blocks = gate_w.reshape(4, nchunk, per, bd, bd)
    blocks = jnp.transpose(blocks, (1, 2, 3, 0, 4))
    same_block = jnp.eye(per, dtype=F32)[None, :, None, None, :, None]
    w = blocks[:, :, :, :, None, :] * same_block
    w = w.reshape(nchunk, LANES, 4 * LANES).astype(BF16)
    b = gate_b.reshape(4, nchunk, LANES).transpose(1, 0, 2).reshape(nchunk, 1, 4 * LANES)
    return w, b


def _rglru_mixer(main, b, conv_w, conv_b, gate_w, gate_b, lam):
    nchunk = D_RG // LANES
    wg, bg = _rglru_gate_weights(gate_w, gate_b)
    main3 = main.reshape(b, SEQ, D_MAIN)
    blk = lambda off: pl.BlockSpec((1, SEQ, LANES), lambda i, c, off=off: (i, 0, off // LANES + c))
    yc = pl.pallas_call(
        _rglru_kernel,
        grid=(b, nchunk),
        in_specs=[blk(MAIN_RG), blk(MAIN_RG_GATE),
                  pl.BlockSpec((conv_w.shape[0], LANES), lambda i, c: (0, c)),
                  pl.BlockSpec((1, LANES), lambda i, c: (0, c)),
                  pl.BlockSpec((1, LANES, 4 * LANES), lambda i, c: (c, 0, 0)),
                  pl.BlockSpec((1, 1, 4 * LANES), lambda i, c: (c, 0, 0)),
                  pl.BlockSpec((2, LANES), lambda i, c: (0, c))],
        out_specs=pl.BlockSpec((1, SEQ, LANES), lambda i, c: (i, 0, c)),
        out_shape=jax.ShapeDtypeStruct((b, SEQ, D_RG), F32),
        scratch_shapes=[pltpu.VMEM((SEQ + RG_HALO * SUBLANES, LANES), F32), pltpu.VMEM((SEQ, LANES), F32)]
                       + [pltpu.VMEM((2, SEQ, LANES), F32)] * 4,
        compiler_params=_params("parallel", "parallel"),
        name="rglru",
    )(main3, main3, conv_w, conv_b.reshape(1, D_RG), wg, bg, lam)
    return yc.reshape(b * SEQ, D_RG)


MERGE_TM = 512


def _layer_norm(z, g, b):
    mu = jnp.mean(z, axis=-1, keepdims=True)
    zc = z - mu
    var = jnp.mean(zc * zc, axis=-1, keepdims=True)
    return zc * lax.rsqrt(var + LN_EPS) * g + b


def _time_major(ref, dil, scratch):
    if dil == 1:
        return ref[0, 0]
    tiles = D_GROUP // LANES
    for r in range(dil):
        for h in range(tiles):
            scratch[h, pl.ds(r, MERGE_TM // dil, stride=dil), :] = ref[0, r, :, h * LANES:(h + 1) * LANES]
    return jnp.concatenate([scratch[h] for h in range(tiles)], axis=1)


def _merge_kernel(*refs, first_tiles):
    x_refs = refs[:len(refs) - 21]
    (ya_ref, o0_ref, o1_ref, o2_ref, l0_ref, l1_ref, l2_ref, yc_ref,
     wg_ref, bgate_ref, wa_ref, wb_ref, wc_ref, wo_ref, g_ref, b_ref, out_ref,
     s0, s1, s2, s3) = refs[len(x_refs):]
    if len(x_refs) == 1:
        x = x_refs[0][...]
    else:
        x = jnp.where(pl.program_id(0) < first_tiles, x_refs[0][...], x_refs[1][...])
    xb = x.astype(BF16)
    d0, d1, d2 = DILATIONS
    o0, l0 = _time_major(o0_ref, d0, s0), _time_major(l0_ref, d0, s0)
    o1, l1 = _time_major(o1_ref, d1, s0), _time_major(l1_ref, d1, s1)
    o2, l2 = _time_major(o2_ref, d2, s2), _time_major(l2_ref, d2, s3)
    lmax = jnp.maximum(jnp.maximum(l0, l1), l2)
    e0, e1, e2 = jnp.exp(l0 - lmax), jnp.exp(l1 - lmax), jnp.exp(l2 - lmax)
    yb = (e0 * o0 + e1 * o1 + e2 * o2) / (e0 + e1 + e2)
    branches = ((ya_ref[...], wa_ref), (yb, wb_ref), (yc_ref[...], wc_ref))
    mixed = None
    for j, (y, w_ref) in enumerate(branches):
        cols = slice(j * D_MODEL, (j + 1) * D_MODEL)
        gate = jax.nn.sigmoid(_dot(xb, wg_ref[:, cols]) + bgate_ref[:, cols])
        term = gate * _dot(y.astype(BF16), w_ref[...])
        mixed = term if mixed is None else mixed + term
    z = DN_ALPHA * x + _dot(mixed.astype(BF16), wo_ref[...])
    out_ref[...] = _layer_norm(z, g_ref[...], b_ref[...])


def _merge_and_project(xs, ya, att, yc, w_gate, b_gate, w_a, w_b, w_c, w_o, ln_g, ln_b):
    tiles = [x.shape[0] // MERGE_TM for x in xs]
    t = sum(tiles) * MERGE_TM
    tps = SEQ // MERGE_TM
    x_specs = [pl.BlockSpec((MERGE_TM, D_MODEL),
                            lambda i, off=sum(tiles[:k]), last=tiles[k] - 1: (jnp.clip(i - off, 0, last), 0))
               for k in range(len(xs))]
    rows = lambda w: pl.BlockSpec((MERGE_TM, w), lambda i: (i, 0))
    grp = lambda d: pl.BlockSpec((1, d, MERGE_TM // d, D_GROUP), lambda i: (i // tps, 0, i % tps, 0))
    const = lambda a: pl.BlockSpec(a.shape, lambda i: (0,) * a.ndim, pipeline_mode=pl.Buffered(1))
    (o0, l0), (o1, l1), (o2, l2) = att
    weights = (w_gate, b_gate.reshape(1, -1), w_a, w_b, w_c, w_o, ln_g.reshape(1, -1), ln_b.reshape(1, -1))
    return pl.pallas_call(
        functools.partial(_merge_kernel, first_tiles=tiles[0]),
        grid=(t // MERGE_TM,),
        in_specs=x_specs + [rows(D_HY)] + [grp(d) for d in DILATIONS] * 2 + [rows(D_RG)]
                 + [const(a) for a in weights],
        out_specs=rows(D_MODEL),
        out_shape=jax.ShapeDtypeStruct((t, D_MODEL), F32),
        scratch_shapes=[pltpu.VMEM((D_GROUP // LANES, MERGE_TM, LANES), F32)] * 4,
        compiler_params=_params("parallel"),
        name="merge_project_norm",
    )(*xs, ya, o0, o1, o2, l0, l1, l2, yc, *weights)


FFN_TM = 512
FFN_CF = 1536


def _ffn_kernel(x_ref, prev_ref, next_ref, wup_ref, cw_ref, cb_ref, wdn_ref, g_ref, b_ref, *out_refs,
                first_tiles):
    i = pl.program_id(0)
    tiles_per_seq = SEQ // FFN_TM
    x = x_ref[...]
    xb = x.astype(BF16)
    has_prev = (i % tiles_per_seq) != 0
    has_next = (i % tiles_per_seq) != tiles_per_seq - 1
    halo = jnp.concatenate([prev_ref[...], next_ref[...]], axis=0).astype(BF16)
    xcat = jnp.concatenate([xb, halo], axis=0)
    row = lax.broadcasted_iota(jnp.int32, (FFN_TM, FFN_CF), 0)
    acc = None
    for c in range(D_FF // FFN_CF):
        cols = slice(c * FFN_CF, (c + 1) * FFN_CF)
        ucols = slice(D_FF + c * FFN_CF, D_FF + (c + 1) * FFN_CF)
        hgx = _dot(xcat, wup_ref[:, cols])
        hg = hgx[:FFN_TM]
        before = jnp.where(has_prev, hgx[FFN_TM + SUBLANES - 1:FFN_TM + SUBLANES, :], 0.0)
        after = jnp.where(has_next, hgx[FFN_TM + SUBLANES:FFN_TM + SUBLANES + 1, :], 0.0)
        up = jnp.where(row == 0, before, pltpu.roll(hg, 1, axis=0))
        dn = jnp.where(row == FFN_TM - 1, after, pltpu.roll(hg, FFN_TM - 1, axis=0))
        w = cw_ref[:, cols]
        conv = up * w[0:1, :] + hg * w[1:2, :] + dn * w[2:3, :] + cb_ref[:, cols]
        act = jax.nn.gelu(conv) * _dot(xb, wup_ref[:, ucols])
        term = _dot(act.astype(BF16), wdn_ref[cols, :])
        acc = term if acc is None else acc + term
    y = _layer_norm(DN_ALPHA * x + acc, g_ref[...], b_ref[...])
    if len(out_refs) == 1:
        out_refs[0][...] = y
    else:
        @pl.when(i < first_tiles)
        def _():
            out_refs[0][...] = y

        @pl.when(i >= first_tiles)
        def _():
            out_refs[1][...] = y


def _ffn(x2d, w_up, conv_w, conv_b, w_down, ln_g, ln_b, split_rows=None):
    t = x2d.shape[0]
    bpt = FFN_TM // SUBLANES
    nblk = t // SUBLANES
    const = lambda a: pl.BlockSpec(a.shape, lambda i: (0,) * a.ndim, pipeline_mode=pl.Buffered(1))
    weights = (w_up, conv_w, conv_b.reshape(1, -1), w_down, ln_g.reshape(1, -1), ln_b.reshape(1, -1))
    if split_rows is None:
        first = t // FFN_TM
        out_specs = [pl.BlockSpec((FFN_TM, D_MODEL), lambda i: (i, 0))]
        out_shape = [jax.ShapeDtypeStruct((t, D_MODEL), F32)]
    else:
        first = split_rows // FFN_TM
        out_specs = [pl.BlockSpec((FFN_TM, D_MODEL), lambda i: (jnp.minimum(i, first - 1), 0)),
                     pl.BlockSpec((FFN_TM, D_MODEL), lambda i: (jnp.maximum(i - first, 0), 0))]
        out_shape = [jax.ShapeDtypeStruct((split_rows, D_MODEL), F32),
                     jax.ShapeDtypeStruct((t - split_rows, D_MODEL), F32)]
    outs = pl.pallas_call(
        functools.partial(_ffn_kernel, first_tiles=first),
        grid=(t // FFN_TM,),
        in_specs=[pl.BlockSpec((FFN_TM, D_MODEL), lambda i: (i, 0)),
                  pl.BlockSpec((SUBLANES, D_MODEL), lambda i: (jnp.maximum(i * bpt - 1, 0), 0)),
                  pl.BlockSpec((SUBLANES, D_MODEL), lambda i: (jnp.minimum((i + 1) * bpt, nblk - 1), 0))]
                 + [const(a) for a in weights],
        out_specs=out_specs,
        out_shape=out_shape,
        compiler_params=_params("arbitrary"),
        name="ffn_norm",
    )(x2d, x2d, x2d, *weights)
    return outs[0] if split_rows is None else outs


def _encoder_layer(xs, l, p, khat, split_rows=None):
    b = sum(x.shape[0] for x in xs) // SEQ
    main, *att = _input_projection(xs, p["w_in"][l], p["hy_conv_w"][l], p["hy_conv_b"][l])
    ya = _hyena_mixer(main, b, khat[l], p["hy_bias"][l])
    att_out = [_dilated_attention_group(a, dil) for a, dil in zip(att, DILATIONS)]
    yc = _rglru_mixer(main, b, p["rg_conv_w"][l], p["rg_conv_b"][l], p["rg_gate_w"][l], p["rg_gate_b"][l],
                      p["rg_lam"][l])
    x2d = _merge_and_project(xs, ya, att_out, yc,
                             p["w_gate"][l], p["b_gate"][l], p["w_br_a"][l], p["w_br_b"][l], p["w_br_c"][l],
                             p["w_o"][l], p["ln1_g"][l], p["ln1_b"][l])
    return _ffn(x2d, p["w_up"][l], p["ffn_conv_w"][l], p["ffn_conv_b"][l], p["w_down"][l],
                p["ln2_g"][l], p["ln2_b"][l], split_rows=split_rows)


def _trunk(x_a, x_b, p):
    khat = _hyena_filter_spectra(p["hy_filt_w1"], p["hy_filt_b1"], p["hy_filt_w2"], p["hy_filt_b2"],
                                 p["hy_filt_w3"], p["hy_filt_b3"], p["hy_filt_freq"])
    xs = (x_a.reshape(-1, D_MODEL), x_b.reshape(-1, D_MODEL))
    rows_a = xs[0].shape[0]
    for l in range(DEPTH):
        split = rows_a if l == DEPTH - 1 else None
        out = _encoder_layer(xs, l, p, khat, split_rows=split)
        xs = tuple(out) if split is not None else (out,)
    return xs[0].reshape(x_a.shape), xs[1].reshape(x_b.shape)


_MATMUL_WEIGHTS = ("w_in", "w_gate", "w_br_a", "w_br_b", "w_br_c", "w_o", "w_up", "w_down")


def kernel(x_prompt, x_sample, w_in, hy_conv_w, hy_conv_b, hy_filt_w1, hy_filt_b1, hy_filt_w2, hy_filt_b2, hy_filt_w3, hy_filt_b3, hy_filt_freq, hy_bias, rg_conv_w, rg_conv_b, rg_gate_w, rg_gate_b, rg_lam, w_gate, b_gate, w_br_a, w_br_b, w_br_c, w_o, ln1_g, ln1_b, w_up, ffn_conv_w, ffn_conv_b, w_down, ln2_g, ln2_b):
    p = dict(w_in=w_in, hy_conv_w=hy_conv_w, hy_conv_b=hy_conv_b, hy_filt_w1=hy_filt_w1,
             hy_filt_b1=hy_filt_b1, hy_filt_w2=hy_filt_w2, hy_filt_b2=hy_filt_b2, hy_filt_w3=hy_filt_w3,
             hy_filt_b3=hy_filt_b3, hy_filt_freq=hy_filt_freq, hy_bias=hy_bias, rg_conv_w=rg_conv_w,
             rg_conv_b=rg_conv_b, rg_gate_w=rg_gate_w, rg_gate_b=rg_gate_b, rg_lam=rg_lam, w_gate=w_gate,
             b_gate=b_gate, w_br_a=w_br_a, w_br_b=w_br_b, w_br_c=w_br_c, w_o=w_o, ln1_g=ln1_g, ln1_b=ln1_b,
             w_up=w_up, ffn_conv_w=ffn_conv_w, ffn_conv_b=ffn_conv_b, w_down=w_down, ln2_g=ln2_g,
             ln2_b=ln2_b)
    for name in _MATMUL_WEIGHTS:
        p[name] = p[name].astype(BF16)
    return _trunk(x_prompt, x_sample, p)
```
